```python
import math, functools
import jax, jax.numpy as jnp
from jax import lax
import numpy as np

D_MODEL = 2048
BATCH = 32
SEQ = 256
DEPTH = 4
DEC_BATCH = 2
DEC_SEQ = 2048
PAST_LEN = 512

GRID_W = 64
POOL_WIDTH = D_MODEL // 2
N_POOL_GROUPS = 4
POOL_GROUP_W = POOL_WIDTH // N_POOL_GROUPS
POOL_WINDOWS = (2, 4, 8, 16)
ATTN_WIDTH = D_MODEL // 2
HEAD_DIM = 64
N_HEADS = ATTN_WIDTH // HEAD_DIM
WIN_ROWS = 8
WIN_COLS = 16
Q_BLOCK = 128
D_FF = ((8 * D_MODEL // 3 + 255) // 256) * 256
N_EXPERTS = 8
TOP_K = 2
D_FF_EXPERT = D_FF
MOE_BLOCK = 256
N_DENSE = (DEPTH + 1) // 2
N_MOE = DEPTH // 2
PROJ_WIDTH = POOL_WIDTH + 3 * ATTN_WIDTH + 2 * D_MODEL
RMS_EPS = 1e-6
NEG_INF = -1e30

kernel_name = 'hybrid_pool_natten_dit_step'


def rmsnorm(x, g):
    xf = x.astype(jnp.float32)
    y = xf * lax.rsqrt(jnp.mean(xf * xf, axis=-1, keepdims=True) + RMS_EPS)
    return (y * g.astype(jnp.float32)).astype(x.dtype)


def adaln(cond, w, b):
    m = jax.nn.silu(cond) @ w + b
    return jnp.split(m[:, None, :], 6, axis=-1)


def modulate(x, shift, scale):
    return x * (1 + scale) + shift


def split_heads(x):
    b, s, _ = x.shape
    return x.reshape(b, s, N_HEADS, HEAD_DIM).transpose(0, 2, 1, 3)


def merge_heads(x):
    b, h, s, d = x.shape
    return x.transpose(0, 2, 1, 3).reshape(b, s, h * d)


def multiscale_pool(u, pool_w, pool_scale):
    b, s, _ = u.shape
    t = jnp.arange(s)
    ug = u.reshape(b, s, N_POOL_GROUPS, POOL_GROUP_W)
    outs = []
    for g, win in enumerate(POOL_WINDOWS):
        xg = ug[:, :, g].astype(jnp.float32)
        cs = jnp.concatenate([jnp.zeros_like(xg[:, :1]), jnp.cumsum(xg, axis=1)], axis=1)
        lo = jnp.clip(t - win // 2, 0, s)
        hi = jnp.clip(t + win - win // 2, 0, s)
        mean = (cs[:, hi] - cs[:, lo]) / (hi - lo).astype(jnp.float32)[None, :, None]
        outs.append((mean - xg).astype(u.dtype))
    pooled = jnp.stack(outs, axis=2)
    mixed = jnp.einsum('bsgc,gcd->bsgd', pooled, pool_w)
    return mixed.reshape(b, s, POOL_WIDTH) * pool_scale


def context_self_attention(q, k, v):
    b, h, s, dh = q.shape
    nb = s // Q_BLOCK
    scale = HEAD_DIM ** -0.5
    qb = jnp.moveaxis(q.reshape(b, h, nb, Q_BLOCK, dh), 2, 0)

    def block(qi):
        sc = jnp.einsum('bhqd,bhkd->bhqk', qi, k).astype(jnp.float32) * scale
        p = jax.nn.softmax(sc, axis=-1).astype(v.dtype)
        return jnp.einsum('bhqk,bhkd->bhqd', p, v)

    o = lax.map(block, qb)
    return jnp.moveaxis(o, 0, 2).reshape(b, h, s, dh)


def neighbourhood_attention(q, k, v, ck, cv, rel_bias):
    b, h, s, dh = q.shape
    rows = s // GRID_W
    kr = min(WIN_ROWS, rows)
    scale = HEAD_DIM ** -0.5
    q5 = q.reshape(b, h, rows, GRID_W, dh)
    k5 = k.reshape(b, h, rows, GRID_W, dh)
    v5 = v.reshape(b, h, rows, GRID_W, dh)
    col = jnp.arange(GRID_W)
    col_start = jnp.clip(col - WIN_COLS // 2, 0, GRID_W - WIN_COLS)
    col_mask = (col[None, :] >= col_start[:, None]) & (col[None, :] < col_start[:, None] + WIN_COLS)
    dc_idx = jnp.clip(col[None, :] - col[:, None], -(WIN_COLS - 1), WIN_COLS - 1) + (WIN_COLS - 1)
    bias_tab = rel_bias.astype(jnp.float32)
    n_loc = kr * GRID_W

    def row(r):
        rs = jnp.clip(r - kr // 2, 0, rows - kr)
        qr = lax.dynamic_index_in_dim(q5, r, axis=2, keepdims=False)
        kb = lax.dynamic_slice_in_dim(k5, rs, kr, axis=2)
        vb = lax.dynamic_slice_in_dim(v5, rs, kr, axis=2)
        dr_idx = rs + jnp.arange(kr) - r + (WIN_ROWS - 1)
        bias = jnp.transpose(bias_tab[:, dr_idx][:, :, dc_idx], (0, 2, 1, 3))
        s_loc = jnp.einsum('bhqd,bhjkd->bhqjk', qr, kb).astype(jnp.float32) * scale + bias
        s_loc = jnp.where(col_mask[:, None, :], s_loc, NEG_INF)
        s_ctx = jnp.einsum('bhqd,bhpd->bhqp', qr, ck).astype(jnp.float32) * scale
        p = jax.nn.softmax(jnp.concatenate([s_loc.reshape(b, h, GRID_W, n_loc), s_ctx], axis=-1), axis=-1).astype(v.dtype)
        o_loc = jnp.einsum('bhqjk,bhjkd->bhqd', p[..., :n_loc].reshape(b, h, GRID_W, kr, GRID_W), vb)
        o_ctx = jnp.einsum('bhqp,bhpd->bhqd', p[..., n_loc:], cv)
        return o_loc + o_ctx

    o = lax.map(row, jnp.arange(rows))
    return jnp.transpose(o, (1, 2, 0, 3, 4)).reshape(b, h, s, dh)


def mixer_inputs(u, w_in, qn_g, kn_g):
    proj = u @ w_in
    cuts = [POOL_WIDTH, POOL_WIDTH + ATTN_WIDTH, POOL_WIDTH + 2 * ATTN_WIDTH,
            POOL_WIDTH + 3 * ATTN_WIDTH, POOL_WIDTH + 3 * ATTN_WIDTH + D_MODEL]
    u_pool, q, k, v, g_pool, g_attn = jnp.split(proj, cuts, axis=-1)
    q = rmsnorm(split_heads(q), qn_g)
    k = rmsnorm(split_heads(k), kn_g)
    return u_pool, q, k, split_heads(v), g_pool, g_attn


def mixer_output(pool_out, attn_out, g_pool, g_attn, w_bp, w_ba, w_out):
    merged = (jax.nn.sigmoid(g_pool) * (pool_out @ w_bp)
              + jax.nn.sigmoid(g_attn) * (merge_heads(attn_out) @ w_ba))
    return merged @ w_out


def swiglu(x, w1, w3, w2):
    return (jax.nn.silu(x @ w1) * (x @ w3)) @ w2


def moe_swiglu(x, router_w, router_b, w1, w3, w2):
    shp = x.shape
    xt = x.reshape(-1, shp[-1])
    n_tok = xt.shape[0]
    n_assign = n_tok * TOP_K
    logits = (xt @ router_w + router_b).astype(jnp.float32)
    top_logit, top_expert = lax.top_k(logits, TOP_K)
    gates = jax.nn.softmax(top_logit, axis=-1).astype(x.dtype).reshape(n_assign)
    expert = top_expert.reshape(n_assign).astype(jnp.int32)
    token = jnp.repeat(jnp.arange(n_tok, dtype=jnp.int32), TOP_K)
    order = jnp.argsort(expert * n_assign + jnp.arange(n_assign, dtype=jnp.int32))
    e_sorted = expert[order]
    counts = jnp.bincount(expert, length=N_EXPERTS).astype(jnp.int32)
    padded = (counts + MOE_BLOCK - 1) // MOE_BLOCK * MOE_BLOCK
    start = jnp.cumsum(counts) - counts
    pend = jnp.cumsum(padded)
    pstart = pend - padded
    dest = pstart[e_sorted] + jnp.arange(n_assign, dtype=jnp.int32) - start[e_sorted]
    n_blocks = -(-n_assign // MOE_BLOCK) + N_EXPERTS
    n_slots = n_blocks * MOE_BLOCK
    slot_token = jnp.zeros((n_slots,), jnp.int32).at[dest].set(token[order])
    slot_gate = jnp.zeros((n_slots,), x.dtype).at[dest].set(gates[order])
    block_expert = jnp.minimum(
        jnp.searchsorted(pend, jnp.arange(n_blocks, dtype=jnp.int32) * MOE_BLOCK, side='right'), N_EXPERTS - 1)

    def expert_block(args):
        tok, g, e = args
        xb = xt[tok]
        hb = jax.nn.silu(xb @ w1[e]) * (xb @ w3[e])
        return (hb @ w2[e]) * g[:, None]

    yb = lax.map(expert_block, (slot_token.reshape(n_blocks, MOE_BLOCK),
                                slot_gate.reshape(n_blocks, MOE_BLOCK), block_expert))
    y = jnp.zeros_like(xt).at[slot_token].add(yb.reshape(n_slots, shp[-1]))
    return y.reshape(shp)


def layer(x, cond, attend, ffn, ada_w, ada_b, norm1_g, norm2_g, w_in, qn_g, kn_g,
          pool_w, pool_scale, w_bp, w_ba, w_out):
    sh1, sc1, g1, sh2, sc2, g2 = adaln(cond, ada_w, ada_b)
    u = modulate(rmsnorm(x, norm1_g), sh1, sc1)
    u_pool, q, k, v, g_pool, g_attn = mixer_inputs(u, w_in, qn_g, kn_g)
    mix = mixer_output(multiscale_pool(u_pool, pool_w, pool_scale), attend(q, k, v),
                       g_pool, g_attn, w_bp, w_ba, w_out)
    x = x + g1 * mix
    x = x + g2 * ffn(modulate(rmsnorm(x, norm2_g), sh2, sc2))
    return x, k, v


def setup_inputs(seed: int = 0) -> dict:
    key = jax.random.key(seed)
    ks = jax.random.split(key, 32)

    def nrm(k, shape, scale):
        return jax.random.normal(k, shape, jnp.float32) * scale

    D = D_MODEL
    return {
        'x_prompt': nrm(ks[0], (BATCH, SEQ, D), 1.0),
        'x_sample': nrm(ks[1], (DEC_BATCH, DEC_SEQ, D), 1.0),
        'cache_k': nrm(ks[2], (DEC_BATCH, DEPTH, N_HEADS, PAST_LEN, HEAD_DIM), 1.0),
        'cache_v': nrm(ks[3], (DEC_BATCH, DEPTH, N_HEADS, PAST_LEN, HEAD_DIM), 1.0),
        'c': nrm(ks[4], (DEC_BATCH, D), 1.0),
        'c_ctx': nrm(ks[5], (D,), 1.0),
        'ada_w': nrm(ks[6], (DEPTH, D, 6 * D), 0.5 * D ** -0.5),
        'ada_b': nrm(ks[7], (DEPTH, 6 * D), 0.02),
        'norm1_g': 1.0 + nrm(ks[8], (DEPTH, D), 0.1),
        'norm2_g': 1.0 + nrm(ks[9], (DEPTH, D), 0.1),
        'w_in': nrm(ks[10], (DEPTH, D, PROJ_WIDTH), D ** -0.5),
        'qn_g': 1.0 + nrm(ks[11], (DEPTH, HEAD_DIM), 0.1),
        'kn_g': 1.0 + nrm(ks[12], (DEPTH, HEAD_DIM), 0.1),
        'pool_w': nrm(ks[13], (DEPTH, N_POOL_GROUPS, POOL_GROUP_W, POOL_GROUP_W), POOL_GROUP_W ** -0.5),
        'pool_scale': 1.0 + nrm(ks[14], (DEPTH, POOL_WIDTH), 0.1),
        'rel_bias': nrm(ks[15], (DEPTH, N_HEADS, 2 * WIN_ROWS - 1, 2 * WIN_COLS - 1), 0.5),
        'w_bp': nrm(ks[16], (DEPTH, POOL_WIDTH, D), POOL_WIDTH ** -0.5),
        'w_ba': nrm(ks[17], (DEPTH, ATTN_WIDTH, D), ATTN_WIDTH ** -0.5),
        'w_out': nrm(ks[18], (DEPTH, D, D), D ** -0.5),
        'ffn_w1': nrm(ks[19], (N_DENSE, D, D_FF), D ** -0.5),
        'ffn_w3': nrm(ks[20], (N_DENSE, D, D_FF), D ** -0.5),
        'ffn_w2': nrm(ks[21], (N_DENSE, D_FF, D), D_FF ** -0.5),
        'router_w': nrm(ks[22], (N_MOE, D, N_EXPERTS), D ** -0.5),
        'router_b': nrm(ks[23], (N_MOE, N_EXPERTS), 0.01),
        'moe_w1': nrm(ks[24], (N_MOE, N_EXPERTS, D, D_FF_EXPERT), D ** -0.5),
        'moe_w3': nrm(ks[25], (N_MOE, N_EXPERTS, D, D_FF_EXPERT), D ** -0.5),
        'moe_w2': nrm(ks[26], (N_MOE, N_EXPERTS, D_FF_EXPERT, D), D_FF_EXPERT ** -0.5),
    }


def reference(x_prompt, x_sample, cache_k, cache_v, c, c_ctx, ada_w, ada_b, norm1_g, norm2_g,
              w_in, qn_g, kn_g, pool_w, pool_scale, rel_bias, w_bp, w_ba, w_out,
              ffn_w1, ffn_w3, ffn_w2, router_w, router_b, moe_w1, moe_w3, moe_w2):
    xp, xs = x_prompt, x_sample
    cond_ctx = c_ctx[None, :]
    new_k, new_v = [], []
    for l in range(DEPTH):
        j = l // 2
        if l % 2 == 0:
            ffn = functools.partial(swiglu, w1=ffn_w1[j], w3=ffn_w3[j], w2=ffn_w2[j])
        else:
            ffn = functools.partial(moe_swiglu, router_w=router_w[j], router_b=router_b[j],
                                    w1=moe_w1[j], w3=moe_w3[j], w2=moe_w2[j])
        shared = (ada_w[l], ada_b[l], norm1_g[l], norm2_g[l], w_in[l], qn_g[l], kn_g[l],
                  pool_w[l], pool_scale[l], w_bp[l], w_ba[l], w_out[l])
        xp, k_ctx, v_ctx = layer(xp, cond_ctx, context_self_attention, ffn, *shared)
        new_k.append(k_ctx)
        new_v.append(v_ctx)
        latent_attend = functools.partial(neighbourhood_attention, ck=cache_k[:, l], cv=cache_v[:, l],
                                          rel_bias=rel_bias[l])
        xs, _, _ = layer(xs, c, latent_attend, ffn, *shared)
    new_cache_k = jnp.stack(new_k, axis=1)
    new_cache_v = jnp.stack(new_v, axis=1)
    return (xp, xs, new_cache_k, new_cache_v)
```

```python
import functools

import jax
import jax.numpy as jnp
from jax import lax
from jax.experimental import pallas as pl
from jax.experimental.pallas import tpu as pltpu

F32 = jnp.float32
BF16 = jnp.bfloat16

D_MODEL = 2048
BATCH = 32
SEQ = 256
DEPTH = 4
DEC_BATCH = 2
DEC_SEQ = 2048
PAST_LEN = 512
GRID_W = 64
GRID_ROWS = DEC_SEQ // GRID_W
POOL_WIDTH = D_MODEL // 2
N_POOL_GROUPS = 4
POOL_GROUP_W = POOL_WIDTH // N_POOL_GROUPS
POOL_WINDOWS = (2, 4, 8, 16)
ATTN_WIDTH = D_MODEL // 2
HEAD_DIM = 64
N_HEADS = ATTN_WIDTH // HEAD_DIM
WIN_ROWS = 8
WIN_COLS = 16
D_FF = ((8 * D_MODEL // 3 + 255) // 256) * 256
N_EXPERTS = 8
TOP_K = 2
PROJ_WIDTH = POOL_WIDTH + 3 * ATTN_WIDTH + 2 * D_MODEL
RMS_EPS = 1e-6
NEG_INF = -1e30
ATTN_SCALE = HEAD_DIM ** -0.5

M_CTX = BATCH * SEQ
M_LAT = DEC_BATCH * DEC_SEQ
M_TOK = M_CTX + M_LAT
N_GROUPS = 1 + DEC_BATCH
MOD_ROWS = 8

LANES = 128
V7X_VMEM_BYTES = 64 * 1024 * 1024
VMEM_BUDGET = V7X_VMEM_BYTES - 8 * 1024 * 1024

CAST_ROWS = 256

TM_NORM = 512
TM_IN, TN_IN = 1024, 1024
TM_MIX, TN_MIX = 512, 1024
TM_OUT, TN_OUT = 1024, 1024
TM_FFN, TN_FFN1, TN_FFN2 = 512, 512, 512
TM_COMB = 256
POOL_TILE = 256
POOL_HALO = 8
LAT_QROWS = 8
LAT_QBLK = LAT_QROWS * GRID_W
LAT_KROWS = 16
LAT_KWIN = LAT_KROWS * GRID_W
N_SLOTS_BLOCKS = (M_TOK * TOP_K) // TM_FFN + N_EXPERTS
N_SLOTS = N_SLOTS_BLOCKS * TM_FFN


def _cparams(semantics, vmem_bytes):
    return pltpu.CompilerParams(dimension_semantics=semantics, vmem_limit_bytes=int(vmem_bytes))


def _group_of_tile(m, tm):
    start = m * tm
    return jnp.where(start < M_CTX, 0, 1 + (start - M_CTX) // DEC_SEQ)


def _sigmoid(z):
    return 1.0 / (1.0 + jnp.exp(-z))


def _silu(z):
    return z / (1.0 + jnp.exp(-z))


def _rms(x):
    return x * lax.rsqrt(jnp.mean(x * x, axis=-1, keepdims=True) + RMS_EPS)


def _cast_weight(w_ref, scr_ref, k_rows):
    def body(i, carry):
        r = pl.multiple_of(i * CAST_ROWS, CAST_ROWS)
        scr_ref[pl.ds(r, CAST_ROWS), :] = w_ref[pl.ds(r, CAST_ROWS), :].astype(BF16)
        return carry

    lax.fori_loop(0, k_rows // CAST_ROWS, body, 0)


def _ada_kernel(c_ref, w_ref, b_ref, o_ref):
    s = _silu(c_ref[...]).astype(BF16)
    w = w_ref[0].astype(BF16)
    o_ref[0] = jnp.dot(s, w, preferred_element_type=F32) + b_ref[0]


def _adaln_all(cond8, ada_w, ada_b):
    tn = 1024
    n_out = 6 * D_MODEL
    return pl.pallas_call(
        _ada_kernel,
        out_shape=jax.ShapeDtypeStruct((DEPTH, MOD_ROWS, n_out), F32),
        grid=(DEPTH, n_out // tn),
        in_specs=[
            pl.BlockSpec((MOD_ROWS, D_MODEL), lambda l, n: (0, 0)),
            pl.BlockSpec((1, D_MODEL, tn), lambda l, n: (l, 0, n)),
            pl.BlockSpec((1, 1, tn), lambda l, n: (l, 0, n)),
        ],
        out_specs=pl.BlockSpec((1, MOD_ROWS, tn), lambda l, n: (l, 0, n)),
        compiler_params=_cparams(("arbitrary", "arbitrary"), 40 * 2**20),
        name="adaln",
    )(cond8, ada_w, ada_b.reshape(DEPTH, 1, n_out))


def _norm_mod_kernel(x_ref, g_ref, mod_ref, o_ref, *, row):
    y = _rms(x_ref[...]) * g_ref[...]
    sh = mod_ref[0, row:row + 1, :]
    sc = mod_ref[0, row + 1:row + 2, :]
    o_ref[...] = (y * (1.0 + sc) + sh).astype(o_ref.dtype)


def _norm_mod(x, g, mods, row):
    tm = TM_NORM
    return pl.pallas_call(
        functools.partial(_norm_mod_kernel, row=row),
        out_shape=jax.ShapeDtypeStruct((M_TOK, D_MODEL), BF16),
        grid=(M_TOK // tm,),
        in_specs=[
            pl.BlockSpec((tm, D_MODEL), lambda m: (m, 0)),
            pl.BlockSpec((1, D_MODEL), lambda m: (0, 0)),
            pl.BlockSpec((1, MOD_ROWS, D_MODEL), lambda m: (_group_of_tile(m, tm), 0, 0)),
        ],
        out_specs=pl.BlockSpec((tm, D_MODEL), lambda m: (m, 0)),
        compiler_params=_cparams(("parallel",), 32 * 2**20),
        name="norm_mod",
    )(x, g.reshape(1, D_MODEL), mods)


def _norm_route_kernel(x_ref, g_ref, mod_ref, rw_ref, rb_ref, u_ref, r_ref, *, row):
    y = _rms(x_ref[...]) * g_ref[...]
    sh = mod_ref[0, row:row + 1, :]
    sc = mod_ref[0, row + 1:row + 2, :]
    u = y * (1.0 + sc) + sh
    u_ref[...] = u
    logits = jnp.dot(u, rw_ref[...], preferred_element_type=F32,
                     precision=lax.Precision.HIGHEST) + rb_ref[...]
    lane = lax.broadcasted_iota(jnp.int32, logits.shape, 1).astype(F32)
    big = float(LANES)
    m1 = jnp.max(logits, axis=-1, keepdims=True)
    i1 = jnp.min(jnp.where(logits == m1, lane, big), axis=-1, keepdims=True)
    rest = jnp.where(lane == i1, -jnp.inf, logits)
    m2 = jnp.max(rest, axis=-1, keepdims=True)
    i2 = jnp.min(jnp.where(rest == m2, lane, big), axis=-1, keepdims=True)
    e2 = jnp.exp(m2 - m1)
    den = 1.0 + e2
    r_ref[...] = jnp.where(lane == 0.0, i1,
                           jnp.where(lane == 1.0, i2,
                                     jnp.where(lane == 2.0, 1.0 / den,
                                               jnp.where(lane == 3.0, e2 / den, 0.0))))


def _norm_route(x, g, mods, row, router_w, router_b):
    tm = TM_NORM
    rw = jnp.zeros((D_MODEL, LANES), F32).at[:, :N_EXPERTS].set(router_w)
    rb = jnp.full((1, LANES), -jnp.inf, F32).at[0, :N_EXPERTS].set(router_b)
    return pl.pallas_call(
        functools.partial(_norm_route_kernel, row=row),
        out_shape=(jax.ShapeDtypeStruct((M_TOK, D_MODEL), F32),
                   jax.ShapeDtypeStruct((M_TOK, LANES), F32)),
        grid=(M_TOK // tm,),
        in_specs=[
            pl.BlockSpec((tm, D_MODEL), lambda m: (m, 0)),
            pl.BlockSpec((1, D_MODEL), lambda m: (0, 0)),
            pl.BlockSpec((1, MOD_ROWS, D_MODEL), lambda m: (_group_of_tile(m, tm), 0, 0)),
            pl.BlockSpec((D_MODEL, LANES), lambda m: (0, 0)),
            pl.BlockSpec((1, LANES), lambda m: (0, 0)),
        ],
        out_specs=(pl.BlockSpec((tm, D_MODEL), lambda m: (m, 0)),
                   pl.BlockSpec((tm, LANES), lambda m: (m, 0))),
        compiler_params=_cparams(("parallel",), 40 * 2**20),
        name="norm_route",
    )(x, g.reshape(1, D_MODEL), mods, rw, rb)


def _in_proj_kernel(a_ref, w_ref, o_ref, wscr):
    @pl.when(pl.program_id(1) == 0)
    def _():
        _cast_weight(w_ref.at[0], wscr, D_MODEL)

    o_ref[...] = jnp.dot(a_ref[...], wscr[...], preferred_element_type=F32)


def _in_proj(u, w_in, layer):
    tm, tn = TM_IN, TN_IN
    return pl.pallas_call(
        _in_proj_kernel,
        out_shape=jax.ShapeDtypeStruct((M_TOK, PROJ_WIDTH), F32),
        grid=(PROJ_WIDTH // tn, M_TOK // tm),
        in_specs=[
            pl.BlockSpec((tm, D_MODEL), lambda n, m: (m, 0)),
            pl.BlockSpec((1, D_MODEL, tn), lambda n, m: (layer, 0, n)),
        ],
        out_specs=pl.BlockSpec((tm, tn), lambda n, m: (m, n)),
        scratch_shapes=[pltpu.VMEM((D_MODEL, tn), BF16)],
        compiler_params=_cparams(("arbitrary", "arbitrary"), VMEM_BUDGET),
        name="in_proj",
    )(u, w_in)


def _pool_kernel(xm_ref, xp_ref, xn_ref, pw_ref, ps_ref, o_ref, pad):
    i = pl.program_id(0)
    n_ctx_tiles = M_CTX // POOL_TILE
    tiles_per_lat = DEC_SEQ // POOL_TILE
    is_lat = i >= n_ctx_tiles
    j = (i - n_ctx_tiles) % tiles_per_lat
    has_prev = jnp.logical_and(is_lat, j != 0)
    has_next = jnp.logical_and(is_lat, j != tiles_per_lat - 1)
    pad[0:POOL_HALO, :] = jnp.where(has_prev, xp_ref[...], 0.0)
    pad[POOL_HALO:POOL_HALO + POOL_TILE, :] = xm_ref[...]
    pad[POOL_HALO + POOL_TILE:, :] = jnp.where(has_next, xn_ref[...], 0.0)
    off = jnp.where(is_lat, j * POOL_TILE, 0)
    seq_len = jnp.where(is_lat, DEC_SEQ, SEQ)
    pos = lax.broadcasted_iota(jnp.int32, (POOL_TILE, 1), 0) + off
    for g, win in enumerate(POOL_WINDOWS):
        lanes = slice(g * POOL_GROUP_W, (g + 1) * POOL_GROUP_W)
        back, fwd = win // 2, win - win // 2
        acc = pad[pl.ds(POOL_HALO - back, POOL_TILE), lanes]
        for d in range(-back + 1, fwd):
            acc = acc + pad[pl.ds(POOL_HALO + d, POOL_TILE), lanes]
        cnt = (jnp.minimum(pos + fwd, seq_len) - jnp.maximum(pos - back, 0)).astype(F32)
        pooled = (acc / cnt - xm_ref[:, lanes]).astype(BF16)
        mixed = jnp.dot(pooled, pw_ref[g].astype(BF16), preferred_element_type=F32)
        o_ref[:, lanes] = (mixed * ps_ref[:, lanes]).astype(BF16)


def _pool_mixer(proj, pool_w, pool_scale):
    t, h = POOL_TILE, POOL_HALO
    hb = t // h
    last_halo = M_TOK // h - 1
    return pl.pallas_call(
        _pool_kernel,
        out_shape=jax.ShapeDtypeStruct((M_TOK, POOL_WIDTH), BF16),
        grid=(M_TOK // t,),
        in_specs=[
            pl.BlockSpec((t, POOL_WIDTH), lambda i: (i, 0)),
            pl.BlockSpec((h, POOL_WIDTH), lambda i: (jnp.maximum(i * hb - 1, 0), 0)),
            pl.BlockSpec((h, POOL_WIDTH), lambda i: (jnp.minimum((i + 1) * hb, last_halo), 0)),
            pl.BlockSpec((N_POOL_GROUPS, POOL_GROUP_W, POOL_GROUP_W), lambda i: (0, 0, 0)),
            pl.BlockSpec((1, POOL_WIDTH), lambda i: (0, 0)),
        ],
        out_specs=pl.BlockSpec((t, POOL_WIDTH), lambda i: (i, 0)),
        scratch_shapes=[pltpu.VMEM((t + 2 * h, POOL_WIDTH), F32)],
        compiler_params=_cparams(("parallel",), 24 * 2**20),
        name="pool_mixer",
    )(proj, proj, proj, pool_w, pool_scale.reshape(1, POOL_WIDTH))


def _ctx_attn_kernel(q_ref, k_ref, v_ref, qg_ref, kg_ref, _ck_in, _cv_in, o_ref, ko_ref, vo_ref):
    qg = qg_ref[...] * ATTN_SCALE
    kg = kg_ref[...]
    for h in range(N_HEADS):
        sl = slice(h * HEAD_DIM, (h + 1) * HEAD_DIM)
        kn = _rms(k_ref[:, sl]) * kg
        v = v_ref[:, sl]
        ko_ref[0, 0, h] = kn
        vo_ref[0, 0, h] = v
        qn = (_rms(q_ref[:, sl]) * qg).astype(BF16)
        s = lax.dot_general(qn, kn.astype(BF16), (((1,), (1,)), ((), ())),
                            preferred_element_type=F32)
        e = jnp.exp(s - jnp.max(s, axis=-1, keepdims=True))
        p = (e / jnp.sum(e, axis=-1, keepdims=True)).astype(BF16)
        o = jnp.dot(p, v.astype(BF16), preferred_element_type=F32)
        o_ref[:, sl] = o.astype(BF16)


def _ctx_attention(proj, qn_g, kn_g, layer, new_k, new_v):
    q_blk = POOL_WIDTH // ATTN_WIDTH
    cache_shape = jax.ShapeDtypeStruct((BATCH, DEPTH, N_HEADS, SEQ, HEAD_DIM), F32)
    cache_spec = pl.BlockSpec((1, 1, N_HEADS, SEQ, HEAD_DIM), lambda b: (b, layer, 0, 0, 0))
    return pl.pallas_call(
        _ctx_attn_kernel,
        out_shape=(jax.ShapeDtypeStruct((M_CTX, ATTN_WIDTH), BF16), cache_shape, cache_shape),
        grid=(BATCH,),
        in_specs=[
            pl.BlockSpec((SEQ, ATTN_WIDTH), lambda b: (b, q_blk)),
            pl.BlockSpec((SEQ, ATTN_WIDTH), lambda b: (b, q_blk + 1)),
            pl.BlockSpec((SEQ, ATTN_WIDTH), lambda b: (b, q_blk + 2)),
            pl.BlockSpec((1, HEAD_DIM), lambda b: (0, 0)),
            pl.BlockSpec((1, HEAD_DIM), lambda b: (0, 0)),
            pl.BlockSpec(memory_space=pl.ANY),
            pl.BlockSpec(memory_space=pl.ANY),
        ],
        out_specs=(pl.BlockSpec((SEQ, ATTN_WIDTH), lambda b: (b, 0)), cache_spec, cache_spec),
        input_output_aliases={5: 1, 6: 2},
        compiler_params=_cparams(("arbitrary",), 40 * 2**20),
        name="ctx_attention",
    )(proj, proj, proj, qn_g.reshape(1, HEAD_DIM), kn_g.reshape(1, HEAD_DIM), new_k, new_v)


def _lat_attn_kernel(q_ref, k_ref, v_ref, ck_ref, cv_ref, t2_ref, qg_ref, kg_ref, o_ref,
                     kscr, vscr, sscr):
    blk = pl.program_id(2)

    @pl.when(blk == 0)
    def _():
        for hh in range(2):
            sl = slice(hh * HEAD_DIM, (hh + 1) * HEAD_DIM)
            kscr[hh] = (_rms(k_ref[:, sl]) * kg_ref[...]).astype(BF16)
            vscr[hh] = v_ref[:, sl].astype(BF16)

    qg = qg_ref[...] * ATTN_SCALE
    r0 = blk * LAT_QROWS
    w0 = jnp.clip(r0 - WIN_ROWS // 2, 0, GRID_ROWS - LAT_KROWS)
    kstart = pl.multiple_of(w0 * GRID_W, 256)
    lane_lo = lax.broadcasted_iota(jnp.int32, (GRID_W, 2 * GRID_W), 1) < GRID_W
    outs = []
    for hh in range(2):
        sl = slice(hh * HEAD_DIM, (hh + 1) * HEAD_DIM)
        qn = (_rms(q_ref[:, sl]) * qg).astype(BF16)
        kw = kscr[hh, pl.ds(kstart, LAT_KWIN), :]
        vw = vscr[hh, pl.ds(kstart, LAT_KWIN), :]
        sscr[...] = lax.dot_general(qn, kw, (((1,), (1,)), ((), ())), preferred_element_type=F32)
        for qi in range(LAT_QROWS):
            qr = r0 + qi
            rs = jnp.clip(qr - WIN_ROWS // 2, 0, GRID_ROWS - WIN_ROWS)
            for jp in range(LAT_KROWS // 2):
                kr0 = w0 + 2 * jp
                idx = jnp.clip(kr0 - qr + WIN_ROWS, 0, 2 * WIN_ROWS - 1)
                ok0 = jnp.logical_and(kr0 >= rs, kr0 < rs + WIN_ROWS).astype(jnp.int32)
                ok1 = jnp.logical_and(kr0 + 1 >= rs, kr0 + 1 < rs + WIN_ROWS).astype(jnp.int32)
                ok = jnp.where(lane_lo, ok0, ok1) > 0
                tile = jnp.where(ok, t2_ref[0, hh, idx], NEG_INF)
                rows = slice(qi * GRID_W, (qi + 1) * GRID_W)
                cols = slice(jp * 2 * GRID_W, (jp + 1) * 2 * GRID_W)
                sscr[rows, cols] = sscr[rows, cols] + tile
        s_loc = sscr[...]
        s_ctx = lax.dot_general(qn, ck_ref[0, 0, hh].astype(BF16), (((1,), (1,)), ((), ())),
                                preferred_element_type=F32)
        m = jnp.maximum(jnp.max(s_loc, axis=-1, keepdims=True), jnp.max(s_ctx, axis=-1, keepdims=True))
        e_loc = jnp.exp(s_loc - m)
        e_ctx = jnp.exp(s_ctx - m)
        den = jnp.sum(e_loc, axis=-1, keepdims=True) + jnp.sum(e_ctx, axis=-1, keepdims=True)
        o_loc = jnp.dot((e_loc / den).astype(BF16), vw, preferred_element_type=F32)
        o_ctx = jnp.dot((e_ctx / den).astype(BF16), cv_ref[0, 0, hh].astype(BF16),
                        preferred_element_type=F32)
        outs.append(o_loc + o_ctx)
    o_ref[...] = jnp.concatenate(outs, axis=-1).astype(BF16)


def _bias_tables(rel_bias):
    col = jnp.arange(GRID_W)
    col_start = jnp.clip(col - WIN_COLS // 2, 0, GRID_W - WIN_COLS)
    col_mask = (col[None, :] >= col_start[:, None]) & (col[None, :] < col_start[:, None] + WIN_COLS)
    dc_idx = jnp.clip(col[None, :] - col[:, None], -(WIN_COLS - 1), WIN_COLS - 1) + (WIN_COLS - 1)
    t = jnp.where(col_mask, rel_bias.astype(F32)[:, :, :, dc_idx], NEG_INF)
    neg = jnp.full(t.shape[:2] + (1,) + t.shape[3:], NEG_INF, F32)
    tp = jnp.concatenate([neg, t, neg], axis=2)
    return jnp.concatenate([tp[:, :, :2 * WIN_ROWS], tp[:, :, 1:2 * WIN_ROWS + 1]], axis=-1)


def _lat_attention(proj, cache_k, cache_v, t2, qn_g, kn_g, layer):
    hp_w = 2 * HEAD_DIM
    q_col = POOL_WIDTH // hp_w
    k_col = (POOL_WIDTH + ATTN_WIDTH) // hp_w
    v_col = (POOL_WIDTH + 2 * ATTN_WIDTH) // hp_w
    lat_row_seq = M_CTX // DEC_SEQ
    lat_row_blk = M_CTX // LAT_QBLK
    n_blk = DEC_SEQ // LAT_QBLK
    return pl.pallas_call(
        _lat_attn_kernel,
        out_shape=jax.ShapeDtypeStruct((M_LAT, ATTN_WIDTH), BF16),
        grid=(DEC_BATCH, N_HEADS // 2, n_blk),
        in_specs=[
            pl.BlockSpec((LAT_QBLK, hp_w), lambda b, hp, blk: (lat_row_blk + b * n_blk + blk, q_col + hp)),
            pl.BlockSpec((DEC_SEQ, hp_w), lambda b, hp, blk: (lat_row_seq + b, k_col + hp)),
            pl.BlockSpec((DEC_SEQ, hp_w), lambda b, hp, blk: (lat_row_seq + b, v_col + hp)),
            pl.BlockSpec((1, 1, 2, PAST_LEN, HEAD_DIM), lambda b, hp, blk: (b, layer, hp, 0, 0)),
            pl.BlockSpec((1, 1, 2, PAST_LEN, HEAD_DIM), lambda b, hp, blk: (b, layer, hp, 0, 0)),
            pl.BlockSpec((1, 2, 2 * WIN_ROWS, GRID_W, 2 * GRID_W), lambda b, hp, blk: (layer, hp, 0, 0, 0)),
            pl.BlockSpec((1, HEAD_DIM), lambda b, hp, blk: (0, 0)),
            pl.BlockSpec((1, HEAD_DIM), lambda b, hp, blk: (0, 0)),
        ],
        out_specs=pl.BlockSpec((LAT_QBLK, hp_w), lambda b, hp, blk: (b * n_blk + blk, hp)),
        scratch_shapes=[
            pltpu.VMEM((2, DEC_SEQ, HEAD_DIM), BF16),
            pltpu.VMEM((2, DEC_SEQ, HEAD_DIM), BF16),
            pltpu.VMEM((LAT_QBLK, LAT_KWIN), F32),
        ],
        compiler_params=_cparams(("arbitrary", "arbitrary", "arbitrary"), 48 * 2**20),
        name="lat_attention",
    )(proj, proj, proj, cache_k, cache_v, t2, qn_g.reshape(1, HEAD_DIM), kn_g.reshape(1, HEAD_DIM))


def _mix_kernel(p_ref, ac_ref, al_ref, wbp_ref, wba_ref, gp_ref, ga_ref, o_ref, sbp, sba):
    m = pl.program_id(1)

    @pl.when(m == 0)
    def _():
        _cast_weight(wbp_ref.at[0], sbp, POOL_WIDTH)
        _cast_weight(wba_ref.at[0], sba, ATTN_WIDTH)

    a = jnp.where(m < M_CTX // TM_MIX, ac_ref[...], al_ref[...])
    yp = jnp.dot(p_ref[...], sbp[...], preferred_element_type=F32)
    ya = jnp.dot(a, sba[...], preferred_element_type=F32)
    o_ref[...] = (_sigmoid(gp_ref[...]) * yp + _sigmoid(ga_ref[...]) * ya).astype(BF16)


def _mix_merge(pool_out, attn_ctx, attn_lat, w_bp, w_ba, proj, layer):
    tm, tn = TM_MIX, TN_MIX
    gp_col = (POOL_WIDTH + 3 * ATTN_WIDTH) // tn
    ga_col = gp_col + D_MODEL // tn
    n_ctx = M_CTX // tm
    return pl.pallas_call(
        _mix_kernel,
        out_shape=jax.ShapeDtypeStruct((M_TOK, D_MODEL), BF16),
        grid=(D_MODEL // tn, M_TOK // tm),
        in_specs=[
            pl.BlockSpec((tm, POOL_WIDTH), lambda n, m: (m, 0)),
            pl.BlockSpec((tm, ATTN_WIDTH), lambda n, m: (jnp.minimum(m, n_ctx - 1), 0)),
            pl.BlockSpec((tm, ATTN_WIDTH), lambda n, m: (jnp.maximum(m - n_ctx, 0), 0)),
            pl.BlockSpec((1, POOL_WIDTH, tn), lambda n, m: (layer, 0, n)),
            pl.BlockSpec((1, ATTN_WIDTH, tn), lambda n, m: (layer, 0, n)),
            pl.BlockSpec((tm, tn), lambda n, m: (m, gp_col + n)),
            pl.BlockSpec((tm, tn), lambda n, m: (m, ga_col + n)),
        ],
        out_specs=pl.BlockSpec((tm, tn), lambda n, m: (m, n)),
        scratch_shapes=[pltpu.VMEM((POOL_WIDTH, tn), BF16), pltpu.VMEM((ATTN_WIDTH, tn), BF16)],
        compiler_params=_cparams(("arbitrary", "arbitrary"), 48 * 2**20),
        name="mix_merge",
    )(pool_out, attn_ctx, attn_lat, w_bp, w_ba, proj, proj)


def _out_proj_kernel(a_ref, w_ref, x_ref, mod_ref, o_ref, wscr, *, row):
    @pl.when(pl.program_id(1) == 0)
    def _():
        _cast_weight(w_ref.at[0], wscr, D_MODEL)

    y = jnp.dot(a_ref[...], wscr[...], preferred_element_type=F32)
    o_ref[...] = x_ref[...] + mod_ref[0, row:row + 1, :] * y


def _out_proj_residual(merged, w_out, layer, x, mods, row):
    tm, tn = TM_OUT, TN_OUT
    return pl.pallas_call(
        functools.partial(_out_proj_kernel, row=row),
        out_shape=jax.ShapeDtypeStruct((M_TOK, D_MODEL), F32),
        grid=(D_MODEL // tn, M_TOK // tm),
        in_specs=[
            pl.BlockSpec((tm, D_MODEL), lambda n, m: (m, 0)),
            pl.BlockSpec((1, D_MODEL, tn), lambda n, m: (layer, 0, n)),
            pl.BlockSpec((tm, tn), lambda n, m: (m, n)),
            pl.BlockSpec((1, MOD_ROWS, tn), lambda n, m: (_group_of_tile(m, tm), 0, n)),
        ],
        out_specs=pl.BlockSpec((tm, tn), lambda n, m: (m, n)),
        scratch_shapes=[pltpu.VMEM((D_MODEL, tn), BF16)],
        compiler_params=_cparams(("arbitrary", "arbitrary"), VMEM_BUDGET),
        name="out_proj",
    )(merged, w_out, x, mods)


def _first_block_of_expert(be_ref, i):
    prev = be_ref[jnp.maximum(i - 1, 0)]
    return jnp.logical_or(i == 0, be_ref[i] != prev)


def _ffn1_kernel(be_ref, nv_ref, a_ref, w1_ref, w3_ref, o_ref, s1, s3):
    i = pl.program_id(1)

    @pl.when(_first_block_of_expert(be_ref, i))
    def _():
        _cast_weight(w1_ref.at[0, 0], s1, D_MODEL)
        _cast_weight(w3_ref.at[0, 0], s3, D_MODEL)

    @pl.when(i < nv_ref[0])
    def _():
        a = a_ref[...]
        h1 = jnp.dot(a, s1[...], preferred_element_type=F32)
        h3 = jnp.dot(a, s3[...], preferred_element_type=F32)
        o_ref[...] = (_silu(h1) * h3).astype(BF16)

    @pl.when(i >= nv_ref[0])
    def _():
        o_ref[...] = jnp.zeros_like(o_ref)


def _ffn_up(a, w1, w3, j, block_expert, n_valid, tm):
    rows = a.shape[0]
    tn = TN_FFN1
    grid_spec = pltpu.PrefetchScalarGridSpec(
        num_scalar_prefetch=2,
        grid=(D_FF // tn, rows // tm),
        in_specs=[
            pl.BlockSpec((tm, D_MODEL), lambda n, i, be, nv: (i, 0)),
            pl.BlockSpec((1, 1, D_MODEL, tn), lambda n, i, be, nv: (j, be[i], 0, n)),
            pl.BlockSpec((1, 1, D_MODEL, tn), lambda n, i, be, nv: (j, be[i], 0, n)),
        ],
        out_specs=pl.BlockSpec((tm, tn), lambda n, i, be, nv: (i, n)),
        scratch_shapes=[pltpu.VMEM((D_MODEL, tn), BF16), pltpu.VMEM((D_MODEL, tn), BF16)],
    )
    return pl.pallas_call(
        _ffn1_kernel,
        out_shape=jax.ShapeDtypeStruct((rows, D_FF), BF16),
        grid_spec=grid_spec,
        compiler_params=_cparams(("arbitrary", "arbitrary"), 48 * 2**20),
        name="ffn_up",
    )(block_expert, n_valid, a, w1, w3)


def _ffn2_dense_kernel(be_ref, nv_ref, a_ref, w_ref, x_ref, mod_ref, o_ref, wscr, *, row):
    i = pl.program_id(1)

    @pl.when(_first_block_of_expert(be_ref, i))
    def _():
        _cast_weight(w_ref.at[0, 0], wscr, D_FF)

    y = jnp.dot(a_ref[...], wscr[...], preferred_element_type=F32)
    o_ref[...] = x_ref[...] + mod_ref[0, row:row + 1, :] * y


def _ffn2_moe_kernel(be_ref, nv_ref, a_ref, w_ref, o_ref, wscr):
    i = pl.program_id(1)

    @pl.when(_first_block_of_expert(be_ref, i))
    def _():
        _cast_weight(w_ref.at[0, 0], wscr, D_FF)

    @pl.when(i < nv_ref[0])
    def _():
        o_ref[...] = jnp.dot(a_ref[...], wscr[...], preferred_element_type=F32)

    @pl.when(i >= nv_ref[0])
    def _():
        o_ref[...] = jnp.zeros_like(o_ref)


def _ffn_down(h, w2, j, block_expert, n_valid, tm, x=None, mods=None, row=None):
    rows = h.shape[0]
    tn = TN_FFN2
    in_specs = [
        pl.BlockSpec((tm, D_FF), lambda n, i, be, nv: (i, 0)),
        pl.BlockSpec((1, 1, D_FF, tn), lambda n, i, be, nv: (j, be[i], 0, n)),
    ]
    args = [h, w2]
    if x is not None:
        kern = functools.partial(_ffn2_dense_kernel, row=row)
        in_specs += [
            pl.BlockSpec((tm, tn), lambda n, i, be, nv: (i, n)),
            pl.BlockSpec((1, MOD_ROWS, tn), lambda n, i, be, nv: (_group_of_tile(i, tm), 0, n)),
        ]
        args += [x, mods]
    else:
        kern = _ffn2_moe_kernel
    grid_spec = pltpu.PrefetchScalarGridSpec(
        num_scalar_prefetch=2,
        grid=(D_MODEL // tn, rows // tm),
        in_specs=in_specs,
        out_specs=pl.BlockSpec((tm, tn), lambda n, i, be, nv: (i, n)),
        scratch_shapes=[pltpu.VMEM((D_FF, tn), BF16)],
    )
    return pl.pallas_call(
        kern,
        out_shape=jax.ShapeDtypeStruct((rows, D_MODEL), F32),
        grid_spec=grid_spec,
        compiler_params=_cparams(("arbitrary", "arbitrary"), VMEM_BUDGET),
        name="ffn_down",
    )(block_expert, n_valid, *args)


def _row_copy(src_hbm, src_row, dst, dst_row, sem):
    return pltpu.make_async_copy(src_hbm.at[pl.ds(src_row, 1), :], dst.at[pl.ds(dst_row, 1), :], sem)


def _gather_kernel(tok_ref, u_hbm, o_ref, buf, sem):
    i = pl.program_id(0)
    nb = pl.num_programs(0)
    tm = o_ref.shape[0]

    def issue(block, slot):
        def body(r, carry):
            _row_copy(u_hbm, tok_ref[block * tm + r], buf.at[slot], r, sem.at[slot]).start()
            return carry

        lax.fori_loop(0, tm, body, 0)

    @pl.when(i == 0)
    def _():
        issue(0, 0)

    @pl.when(i + 1 < nb)
    def _():
        issue(i + 1, (i + 1) % 2)

    slot = i % 2

    def wait_body(r, carry):
        _row_copy(u_hbm, 0, buf.at[slot], r, sem.at[slot]).wait()
        return carry

    lax.fori_loop(0, tm, wait_body, 0)
    o_ref[...] = buf[slot].astype(BF16)


def _gather_rows(u, slot_token):
    tm = TM_FFN
    grid_spec = pltpu.PrefetchScalarGridSpec(
        num_scalar_prefetch=1,
        grid=(N_SLOTS // tm,),
        in_specs=[pl.BlockSpec(memory_space=pl.ANY)],
        out_specs=pl.BlockSpec((tm, D_MODEL), lambda i, tok: (i, 0)),
        scratch_shapes=[pltpu.VMEM((2, tm, D_MODEL), F32), pltpu.SemaphoreType.DMA((2,))],
    )
    return pl.pallas_call(
        _gather_kernel,
        out_shape=jax.ShapeDtypeStruct((N_SLOTS, D_MODEL), BF16),
        grid_spec=grid_spec,
        compiler_params=_cparams(("arbitrary",), 32 * 2**20),
        name="moe_gather",
    )(slot_token, u)


def _combine_kernel(d_ref, y_hbm, x_ref, r_ref, mod_ref, o_ref, buf, sem, *, row):
    i = pl.program_id(0)
    nb = pl.num_programs(0)
    tm = o_ref.shape[0]

    def issue(block, slot):
        def body(r, carry):
            t = block * tm + r
            _row_copy(y_hbm, d_ref[2 * t], buf.at[slot, 0], r, sem.at[slot]).start()
            _row_copy(y_hbm, d_ref[2 * t + 1], buf.at[slot, 1], r, sem.at[slot]).start()
            return carry

        lax.fori_loop(0, tm, body, 0)

    @pl.when(i == 0)
    def _():
        issue(0, 0)

    @pl.when(i + 1 < nb)
    def _():
        issue(i + 1, (i + 1) % 2)

    slot = i % 2

    def wait_body(r, carry):
        _row_copy(y_hbm, 0, buf.at[slot, 0], r, sem.at[slot]).wait()
        _row_copy(y_hbm, 0, buf.at[slot, 1], r, sem.at[slot]).wait()
        return carry

    lax.fori_loop(0, tm, wait_body, 0)
    g1 = r_ref[:, 2:3]
    g2 = r_ref[:, 3:4]
    y = buf[slot, 0] * g1 + buf[slot, 1] * g2
    o_ref[...] = x_ref[...] + mod_ref[0, row:row + 1, :] * y


def _moe_combine(x, y_slots, dest, route, mods, row):
    tm = TM_COMB
    grid_spec = pltpu.PrefetchScalarGridSpec(
        num_scalar_prefetch=1,
        grid=(M_TOK // tm,),
        in_specs=[
            pl.BlockSpec(memory_space=pl.ANY),
            pl.BlockSpec((tm, D_MODEL), lambda i, d: (i, 0)),
            pl.BlockSpec((tm, LANES), lambda i, d: (i, 0)),
            pl.BlockSpec((1, MOD_ROWS, D_MODEL), lambda i, d: (_group_of_tile(i, tm), 0, 0)),
        ],
        out_specs=pl.BlockSpec((tm, D_MODEL), lambda i, d: (i, 0)),
        scratch_shapes=[pltpu.VMEM((2, 2, tm, D_MODEL), F32), pltpu.SemaphoreType.DMA((2,))],
    )
    return pl.pallas_call(
        functools.partial(_combine_kernel, row=row),
        out_shape=jax.ShapeDtypeStruct((M_TOK, D_MODEL), F32),
        grid_spec=grid_spec,
        compiler_params=_cparams(("arbitrary",), 32 * 2**20),
        name="moe_combine",
    )(dest.reshape(-1), y_slots, x, route, mods)


def _moe_plan(route):
    tm = TM_FFN
    expert = route[:, :TOP_K].astype(jnp.int32).reshape(-1)
    onehot = (expert[:, None] == jnp.arange(N_EXPERTS, dtype=jnp.int32)[None, :]).astype(jnp.int32)
    csum = jnp.cumsum(onehot, axis=0)
    rank = jnp.take_along_axis(csum, expert[:, None], axis=1)[:, 0] - 1
    counts = csum[-1]
    padded = (counts + tm - 1) // tm * tm
    pend = jnp.cumsum(padded)
    pstart = pend - padded
    dest = pstart[expert] + rank
    token = jnp.arange(M_TOK * TOP_K, dtype=jnp.int32) // TOP_K
    slot_token = jnp.zeros((N_SLOTS,), jnp.int32).at[dest].set(token)
    block_expert = jnp.minimum(
        jnp.searchsorted(pend, jnp.arange(N_SLOTS_BLOCKS, dtype=jnp.int32) * tm, side='right'),
        N_EXPERTS - 1).astype(jnp.int32)
    n_valid = (pend[-1:] // tm).astype(jnp.int32)
    return slot_token, dest.astype(jnp.int32).reshape(M_TOK, TOP_K), block_expert, n_valid


def kernel(x_prompt, x_sample, cache_k, cache_v, c, c_ctx, ada_w, ada_b, norm1_g, norm2_g, w_in, qn_g, kn_g, pool_w, pool_scale, rel_bias, w_bp, w_ba, w_out, ffn_w1, ffn_w3, ffn_w2, router_w, router_b, moe_w1, moe_w3, moe_w2):
    x = jnp.concatenate([x_prompt.reshape(M_CTX, D_MODEL), x_sample.reshape(M_LAT, D_MODEL)], axis=0)
    cond8 = jnp.zeros((MOD_ROWS, D_MODEL), F32).at[0].set(c_ctx).at[1:1 + DEC_BATCH].set(c)
    mods_all = _adaln_all(cond8, ada_w, ada_b)
    mods_all = mods_all[:, :N_GROUPS].reshape(DEPTH, N_GROUPS, 6, D_MODEL)
    mods_all = jnp.pad(mods_all, ((0, 0), (0, 0), (0, MOD_ROWS - 6), (0, 0)))
    t2_all = _bias_tables(rel_bias)

    dense_blocks = jnp.zeros((M_TOK // TM_FFN,), jnp.int32)
    dense_valid = jnp.full((1,), M_TOK // TM_FFN, jnp.int32)

    new_k = jnp.zeros((BATCH, DEPTH, N_HEADS, SEQ, HEAD_DIM), F32)
    new_v = jnp.zeros((BATCH, DEPTH, N_HEADS, SEQ, HEAD_DIM), F32)
    for l in range(DEPTH):
        j = l // 2
        mods = mods_all[l]
        u = _norm_mod(x, norm1_g[l], mods, 0)
        proj = _in_proj(u, w_in, l)
        pool_out = _pool_mixer(proj, pool_w[l], pool_scale[l])
        attn_ctx, new_k, new_v = _ctx_attention(proj, qn_g[l], kn_g[l], l, new_k, new_v)
        attn_lat = _lat_attention(proj, cache_k, cache_v, t2_all, qn_g[l], kn_g[l], l)
        merged = _mix_merge(pool_out, attn_ctx, attn_lat, w_bp, w_ba, proj, l)
        x = _out_proj_residual(merged, w_out, l, x, mods, 2)
        if l % 2 == 0:
            u2 = _norm_mod(x, norm2_g[l], mods, 3)
            h = _ffn_up(u2, ffn_w1[:, None], ffn_w3[:, None], j, dense_blocks, dense_valid, TM_FFN)
            x = _ffn_down(h, ffn_w2[:, None], j, dense_blocks, dense_valid, TM_FFN, x=x, mods=mods, row=5)
        else:
            u2, route = _norm_route(x, norm2_g[l], mods, 3, router_w[j], router_b[j])
            slot_token, dest, block_expert, n_valid = _moe_plan(route)
            xs = _gather_rows(u2, slot_token)
            h = _ffn_up(xs, moe_w1, moe_w3, j, block_expert, n_valid, TM_FFN)
            y_slots = _ffn_down(h, moe_w2, j, block_expert, n_valid, TM_FFN)
            x = _moe_combine(x, y_slots, dest, route, mods, 5)

    y_prompt = x[:M_CTX].reshape(BATCH, SEQ, D_MODEL)
    y_sample = x[M_CTX:].reshape(DEC_BATCH, DEC_SEQ, D_MODEL)
    return (y_prompt, y_sample, new_k, new_v)
```

```python
import functools

import jax
import jax.numpy as jnp
from jax import lax
from jax.experimental import pallas as pl
from jax.experimental.pallas import tpu as pltpu

F32 = jnp.float32
BF16 = jnp.bfloat16

D_MODEL = 2048
BATCH = 32
SEQ = 256
DEPTH = 4
DEC_BATCH = 2
DEC_SEQ = 2048
PAST_LEN = 512
GRID_W = 64
GRID_ROWS = DEC_SEQ // GRID_W
POOL_WIDTH = D_MODEL // 2
N_POOL_GROUPS = 4
POOL_GROUP_W = POOL_WIDTH // N_POOL_GROUPS
POOL_WINDOWS = (2, 4, 8, 16)
ATTN_WIDTH = D_MODEL // 2
HEAD_DIM = 64
N_HEADS = ATTN_WIDTH // HEAD_DIM
WIN_ROWS = 8
WIN_COLS = 16
D_FF = ((8 * D_MODEL // 3 + 255) // 256) * 256
N_EXPERTS = 8
TOP_K = 2
PROJ_WIDTH = POOL_WIDTH + 3 * ATTN_WIDTH + 2 * D_MODEL
RMS_EPS = 1e-6
NEG_INF = -1e30
ATTN_SCALE = HEAD_DIM ** -0.5

M_CTX = BATCH * SEQ
M_LAT = DEC_BATCH * DEC_SEQ
M_TOK = M_CTX + M_LAT
N_GROUPS = 1 + DEC_BATCH
MOD_ROWS = 8

LANES = 128
V7X_VMEM_BYTES = 64 * 1024 * 1024
VMEM_BUDGET = V7X_VMEM_BYTES - 8 * 1024 * 1024

CAST_ROWS = 256

TM_NORM = 512
TM_IN, TN_IN = 1024, 1024
TM_MIX, TN_MIX = 512, 1024
TM_OUT, TN_OUT = 1024, 1024
TM_FFN, TN_FFN1, TN_FFN2 = 512, 512, 512
TM_FFN_UP_DENSE = 1024
TM_COMB = 256
POOL_TILE = 256
POOL_HALO = 8
LAT_QROWS = 8
LAT_QBLK = LAT_QROWS * GRID_W
LAT_KROWS = 16
LAT_KWIN = LAT_KROWS * GRID_W
N_SLOTS_BLOCKS = (M_TOK * TOP_K) // TM_FFN + N_EXPERTS
N_SLOTS = N_SLOTS_BLOCKS * TM_FFN


def _cparams(semantics, vmem_bytes):
    return pltpu.CompilerParams(dimension_semantics=semantics, vmem_limit_bytes=int(vmem_bytes))


def _group_of_tile(m, tm):
    start = m * tm
    return jnp.where(start < M_CTX, 0, 1 + (start - M_CTX) // DEC_SEQ)


def _sigmoid(z):
    return 1.0 / (1.0 + jnp.exp(-z))


def _silu(z):
    return z / (1.0 + jnp.exp(-z))


def _rms(x):
    return x * lax.rsqrt(jnp.mean(x * x, axis=-1, keepdims=True) + RMS_EPS)


def _cast_weight(w_ref, scr_ref, k_rows):
    def body(i, carry):
        r = pl.multiple_of(i * CAST_ROWS, CAST_ROWS)
        scr_ref[pl.ds(r, CAST_ROWS), :] = w_ref[pl.ds(r, CAST_ROWS), :].astype(BF16)
        return carry

    lax.fori_loop(0, k_rows // CAST_ROWS, body, 0)


def _ada_kernel(c_ref, w_ref, b_ref, o_ref):
    s = _silu(c_ref[...]).astype(BF16)
    w = w_ref[0].astype(BF16)
    o_ref[0] = jnp.dot(s, w, preferred_element_type=F32) + b_ref[0]


def _adaln_all(cond8, ada_w, ada_b):
    tn = 1024
    n_out = 6 * D_MODEL
    return pl.pallas_call(
        _ada_kernel,
        out_shape=jax.ShapeDtypeStruct((DEPTH, MOD_ROWS, n_out), F32),
        grid=(DEPTH, n_out // tn),
        in_specs=[
            pl.BlockSpec((MOD_ROWS, D_MODEL), lambda l, n: (0, 0)),
            pl.BlockSpec((1, D_MODEL, tn), lambda l, n: (l, 0, n)),
            pl.BlockSpec((1, 1, tn), lambda l, n: (l, 0, n)),
        ],
        out_specs=pl.BlockSpec((1, MOD_ROWS, tn), lambda l, n: (l, 0, n)),
        compiler_params=_cparams(("arbitrary", "arbitrary"), 40 * 2**20),
        name="adaln",
    )(cond8, ada_w, ada_b.reshape(DEPTH, 1, n_out))


def _norm_mod_kernel(x_ref, g_ref, mod_ref, o_ref, *, row):
    y = _rms(x_ref[...]) * g_ref[...]
    sh = mod_ref[0, row:row + 1, :]
    sc = mod_ref[0, row + 1:row + 2, :]
    o_ref[...] = (y * (1.0 + sc) + sh).astype(o_ref.dtype)


def _norm_mod(x, g, mods, row):
    tm = TM_NORM
    return pl.pallas_call(
        functools.partial(_norm_mod_kernel, row=row),
        out_shape=jax.ShapeDtypeStruct((M_TOK, D_MODEL), BF16),
        grid=(M_TOK // tm,),
        in_specs=[
            pl.BlockSpec((tm, D_MODEL), lambda m: (m, 0)),
            pl.BlockSpec((1, D_MODEL), lambda m: (0, 0)),
            pl.BlockSpec((1, MOD_ROWS, D_MODEL), lambda m: (_group_of_tile(m, tm), 0, 0)),
        ],
        out_specs=pl.BlockSpec((tm, D_MODEL), lambda m: (m, 0)),
        compiler_params=_cparams(("parallel",), 32 * 2**20),
        name="norm_mod",
    )(x, g.reshape(1, D_MODEL), mods)


def _norm_route_kernel(x_ref, g_ref, mod_ref, rw_ref, rb_ref, u_ref, r_ref, *, row):
    y = _rms(x_ref[...]) * g_ref[...]
    sh = mod_ref[0, row:row + 1, :]
    sc = mod_ref[0, row + 1:row + 2, :]
    u = y * (1.0 + sc) + sh
    u_ref[...] = u
    logits = jnp.dot(u, rw_ref[...], preferred_element_type=F32,
                     precision=lax.Precision.HIGHEST) + rb_ref[...]
    lane = lax.broadcasted_iota(jnp.int32, logits.shape, 1).astype(F32)
    big = float(LANES)
    m1 = jnp.max(logits, axis=-1, keepdims=True)
    i1 = jnp.min(jnp.where(logits == m1, lane, big), axis=-1, keepdims=True)
    rest = jnp.where(lane == i1, -jnp.inf, logits)
    m2 = jnp.max(rest, axis=-1, keepdims=True)
    i2 = jnp.min(jnp.where(rest == m2, lane, big), axis=-1, keepdims=True)
    e2 = jnp.exp(m2 - m1)
    den = 1.0 + e2
    r_ref[...] = jnp.where(lane == 0.0, i1,
                           jnp.where(lane == 1.0, i2,
                                     jnp.where(lane == 2.0, 1.0 / den,
                                               jnp.where(lane == 3.0, e2 / den, 0.0))))


def _norm_route(x, g, mods, row, router_w, router_b):
    tm = TM_NORM
    rw = jnp.zeros((D_MODEL, LANES), F32).at[:, :N_EXPERTS].set(router_w)
    rb = jnp.full((1, LANES), -jnp.inf, F32).at[0, :N_EXPERTS].set(router_b)
    return pl.pallas_call(
        functools.partial(_norm_route_kernel, row=row),
        out_shape=(jax.ShapeDtypeStruct((M_TOK, D_MODEL), F32),
                   jax.ShapeDtypeStruct((M_TOK, LANES), F32)),
        grid=(M_TOK // tm,),
        in_specs=[
            pl.BlockSpec((tm, D_MODEL), lambda m: (m, 0)),
            pl.BlockSpec((1, D_MODEL), lambda m: (0, 0)),
            pl.BlockSpec((1, MOD_ROWS, D_MODEL), lambda m: (_group_of_tile(m, tm), 0, 0)),
            pl.BlockSpec((D_MODEL, LANES), lambda m: (0, 0)),
            pl.BlockSpec((1, LANES), lambda m: (0, 0)),
        ],
        out_specs=(pl.BlockSpec((tm, D_MODEL), lambda m: (m, 0)),
                   pl.BlockSpec((tm, LANES), lambda m: (m, 0))),
        compiler_params=_cparams(("parallel",), 40 * 2**20),
        name="norm_route",
    )(x, g.reshape(1, D_MODEL), mods, rw, rb)


def _in_proj_kernel(a_ref, w_ref, o_ref, wscr):
    @pl.when(pl.program_id(1) == 0)
    def _():
        _cast_weight(w_ref.at[0], wscr, D_MODEL)

    o_ref[...] = jnp.dot(a_ref[...], wscr[...], preferred_element_type=F32)


def _in_proj(u, w_in, layer):
    tm, tn = TM_IN, TN_IN
    return pl.pallas_call(
        _in_proj_kernel,
        out_shape=jax.ShapeDtypeStruct((M_TOK, PROJ_WIDTH), F32),
        grid=(PROJ_WIDTH // tn, M_TOK // tm),
        in_specs=[
            pl.BlockSpec((tm, D_MODEL), lambda n, m: (m, 0)),
            pl.BlockSpec((1, D_MODEL, tn), lambda n, m: (layer, 0, n)),
        ],
        out_specs=pl.BlockSpec((tm, tn), lambda n, m: (m, n)),
        scratch_shapes=[pltpu.VMEM((D_MODEL, tn), BF16)],
        compiler_params=_cparams(("arbitrary", "arbitrary"), VMEM_BUDGET),
        name="in_proj",
    )(u, w_in)


def _pool_kernel(xm_ref, xp_ref, xn_ref, pw_ref, ps_ref, o_ref, pad):
    i = pl.program_id(0)
    n_ctx_tiles = M_CTX // POOL_TILE
    tiles_per_lat = DEC_SEQ // POOL_TILE
    is_lat = i >= n_ctx_tiles
    j = (i - n_ctx_tiles) % tiles_per_lat
    has_prev = jnp.logical_and(is_lat, j != 0)
    has_next = jnp.logical_and(is_lat, j != tiles_per_lat - 1)
    pad[0:POOL_HALO, :] = jnp.where(has_prev, xp_ref[...], 0.0)
    pad[POOL_HALO:POOL_HALO + POOL_TILE, :] = xm_ref[...]
    pad[POOL_HALO + POOL_TILE:, :] = jnp.where(has_next, xn_ref[...], 0.0)
    off = jnp.where(is_lat, j * POOL_TILE, 0)
    seq_len = jnp.where(is_lat, DEC_SEQ, SEQ)
    pos = lax.broadcasted_iota(jnp.int32, (POOL_TILE, 1), 0) + off
    for g, win in enumerate(POOL_WINDOWS):
        lanes = slice(g * POOL_GROUP_W, (g + 1) * POOL_GROUP_W)
        back, fwd = win // 2, win - win // 2
        acc = pad[pl.ds(POOL_HALO - back, POOL_TILE), lanes]
        for d in range(-back + 1, fwd):
            acc = acc + pad[pl.ds(POOL_HALO + d, POOL_TILE), lanes]
        cnt = (jnp.minimum(pos + fwd, seq_len) - jnp.maximum(pos - back, 0)).astype(F32)
        pooled = (acc / cnt - xm_ref[:, lanes]).astype(BF16)
        mixed = jnp.dot(pooled, pw_ref[g].astype(BF16), preferred_element_type=F32)
        o_ref[:, lanes] = (mixed * ps_ref[:, lanes]).astype(BF16)


def _pool_mixer(proj, pool_w, pool_scale):
    t, h = POOL_TILE, POOL_HALO
    hb = t // h
    last_halo = M_TOK // h - 1
    return pl.pallas_call(
        _pool_kernel,
        out_shape=jax.ShapeDtypeStruct((M_TOK, POOL_WIDTH), BF16),
        grid=(M_TOK // t,),
        in_specs=[
            pl.BlockSpec((t, POOL_WIDTH), lambda i: (i, 0)),
            pl.BlockSpec((h, POOL_WIDTH), lambda i: (jnp.maximum(i * hb - 1, 0), 0)),
            pl.BlockSpec((h, POOL_WIDTH), lambda i: (jnp.minimum((i + 1) * hb, last_halo), 0)),
            pl.BlockSpec((N_POOL_GROUPS, POOL_GROUP_W, POOL_GROUP_W), lambda i: (0, 0, 0)),
            pl.BlockSpec((1, POOL_WIDTH), lambda i: (0, 0)),
        ],
        out_specs=pl.BlockSpec((t, POOL_WIDTH), lambda i: (i, 0)),
        scratch_shapes=[pltpu.VMEM((t + 2 * h, POOL_WIDTH), F32)],
        compiler_params=_cparams(("parallel",), 24 * 2**20),
        name="pool_mixer",
    )(proj, proj, proj, pool_w, pool_scale.reshape(1, POOL_WIDTH))


PAIR_W = 2 * HEAD_DIM


def _pair_group_ones():
    r = lax.broadcasted_iota(jnp.int32, (PAIR_W, PAIR_W), 0) // HEAD_DIM
    c = lax.broadcasted_iota(jnp.int32, (PAIR_W, PAIR_W), 1) // HEAD_DIM
    return jnp.where(r == c, 1.0, 0.0).astype(BF16)


def _head_masks():
    lane = lax.broadcasted_iota(jnp.int32, (1, PAIR_W), 1)
    return (lane < HEAD_DIM, lane >= HEAD_DIM)


def _rms_pair(x, ones):
    y2 = x * x
    hi = y2.astype(BF16)
    lo = (y2 - hi.astype(F32)).astype(BF16)
    ss = jnp.dot(hi, ones, preferred_element_type=F32) + jnp.dot(lo, ones, preferred_element_type=F32)
    return x * lax.rsqrt(ss * (1.0 / HEAD_DIM) + RMS_EPS)


def _softmax_rows(s):
    e = jnp.exp(s - jnp.max(s, axis=-1, keepdims=True))
    return e * (1.0 / jnp.sum(e, axis=-1, keepdims=True))


_NT_DIMS = (((1,), (1,)), ((), ()))


def _ctx_attn_kernel(q_ref, k_ref, v_ref, qg_ref, kg_ref, _ck_in, _cv_in, o_ref, ko_ref, vo_ref):
    qg = qg_ref[...] * ATTN_SCALE
    kg = kg_ref[...]
    ones = _pair_group_ones()
    masks = _head_masks()
    for p in range(N_HEADS // 2):
        sl = slice(p * PAIR_W, (p + 1) * PAIR_W)
        kn = _rms_pair(k_ref[:, sl], ones) * kg
        v = v_ref[:, sl]
        for hh in range(2):
            lanes = slice(hh * HEAD_DIM, (hh + 1) * HEAD_DIM)
            ko_ref[0, 0, 2 * p + hh] = kn[:, lanes]
            vo_ref[0, 0, 2 * p + hh] = v[:, lanes]
        qn = _rms_pair(q_ref[:, sl], ones) * qg
        kb = kn.astype(BF16)
        o = jnp.zeros((SEQ, PAIR_W), F32)
        for hh in range(2):
            qh = jnp.where(masks[hh], qn, 0.0).astype(BF16)
            s = lax.dot_general(qh, kb, _NT_DIMS, preferred_element_type=F32)
            pr = _softmax_rows(s).astype(BF16)
            vh = jnp.where(masks[hh], v, 0.0).astype(BF16)
            o = o + jnp.dot(pr, vh, preferred_element_type=F32)
        o_ref[:, sl] = o.astype(BF16)


def _pair_gain(g):
    return jnp.concatenate([g, g]).reshape(1, PAIR_W)


def _ctx_attention(proj, qn_g, kn_g, layer, new_k, new_v):
    q_blk = POOL_WIDTH // ATTN_WIDTH
    cache_shape = jax.ShapeDtypeStruct((BATCH, DEPTH, N_HEADS, SEQ, HEAD_DIM), F32)
    cache_spec = pl.BlockSpec((1, 1, N_HEADS, SEQ, HEAD_DIM), lambda b: (b, layer, 0, 0, 0))
    return pl.pallas_call(
        _ctx_attn_kernel,
        out_shape=(jax.ShapeDtypeStruct((M_CTX, ATTN_WIDTH), BF16), cache_shape, cache_shape),
        grid=(BATCH,),
        in_specs=[
            pl.BlockSpec((SEQ, ATTN_WIDTH), lambda b: (b, q_blk)),
            pl.BlockSpec((SEQ, ATTN_WIDTH), lambda b: (b, q_blk + 1)),
            pl.BlockSpec((SEQ, ATTN_WIDTH), lambda b: (b, q_blk + 2)),
            pl.BlockSpec((1, PAIR_W), lambda b: (0, 0)),
            pl.BlockSpec((1, PAIR_W), lambda b: (0, 0)),
            pl.BlockSpec(memory_space=pl.ANY),
            pl.BlockSpec(memory_space=pl.ANY),
        ],
        out_specs=(pl.BlockSpec((SEQ, ATTN_WIDTH), lambda b: (b, 0)), cache_spec, cache_spec),
        input_output_aliases={5: 1, 6: 2},
        compiler_params=_cparams(("arbitrary",), 40 * 2**20),
        name="ctx_attention",
    )(proj, proj, proj, _pair_gain(qn_g), _pair_gain(kn_g), new_k, new_v)


def _lat_attn_kernel(q_ref, k_ref, v_ref, ck_ref, cv_ref, t2_ref, qg_ref, kg_ref, o_ref,
                     kscr, vscr, ckscr, cvscr, sscr):
    blk = pl.program_id(2)

    ones = _pair_group_ones()
    masks = _head_masks()

    @pl.when(blk == 0)
    def _():
        kscr[...] = (_rms_pair(k_ref[...], ones) * kg_ref[...]).astype(BF16)
        ckscr[...] = jnp.concatenate([ck_ref[0, 0, 0], ck_ref[0, 0, 1]], axis=-1).astype(BF16)
        v = v_ref[...]
        cv = jnp.concatenate([cv_ref[0, 0, 0], cv_ref[0, 0, 1]], axis=-1)
        for hh in range(2):
            vscr[hh] = jnp.where(masks[hh], v, 0.0).astype(BF16)
            cvscr[hh] = jnp.where(masks[hh], cv, 0.0).astype(BF16)

    r0 = blk * LAT_QROWS
    w0 = jnp.clip(r0 - WIN_ROWS // 2, 0, GRID_ROWS - LAT_KROWS)
    kstart = pl.multiple_of(w0 * GRID_W, 256)
    lane_lo = lax.broadcasted_iota(jnp.int32, (GRID_W, 2 * GRID_W), 1) < GRID_W
    qn = _rms_pair(q_ref[...], ones) * (qg_ref[...] * ATTN_SCALE)
    kw = kscr[pl.ds(kstart, LAT_KWIN), :]
    ckb = ckscr[...]
    o = jnp.zeros((LAT_QBLK, PAIR_W), F32)
    for hh in range(2):
        qh = jnp.where(masks[hh], qn, 0.0).astype(BF16)
        s = lax.dot_general(qh, kw, _NT_DIMS, preferred_element_type=F32)
        for qi in range(LAT_QROWS):
            qr = r0 + qi
            rs = jnp.clip(qr - WIN_ROWS // 2, 0, GRID_ROWS - WIN_ROWS)
            for jp in range(LAT_KROWS // 2):
                kr0 = w0 + 2 * jp
                idx = jnp.clip(kr0 - qr + WIN_ROWS, 0, 2 * WIN_ROWS - 1)
                ok0 = jnp.logical_and(kr0 >= rs, kr0 < rs + WIN_ROWS).astype(jnp.int32)
                ok1 = jnp.logical_and(kr0 + 1 >= rs, kr0 + 1 < rs + WIN_ROWS).astype(jnp.int32)
                ok = jnp.where(lane_lo, ok0, ok1) > 0
                tile = jnp.where(ok, t2_ref[0, hh, idx], NEG_INF)
                rows = slice(qi * GRID_W, (qi + 1) * GRID_W)
                cols = slice(jp * 2 * GRID_W, (jp + 1) * 2 * GRID_W)
                sscr[rows, cols] = s[rows, cols] + tile
        s_loc = sscr[...]
        s_ctx = lax.dot_general(qh, ckb, _NT_DIMS, preferred_element_type=F32)
        m = jnp.maximum(jnp.max(s_loc, axis=-1, keepdims=True), jnp.max(s_ctx, axis=-1, keepdims=True))
        e_loc = jnp.exp(s_loc - m)
        e_ctx = jnp.exp(s_ctx - m)
        inv = 1.0 / (jnp.sum(e_loc, axis=-1, keepdims=True) + jnp.sum(e_ctx, axis=-1, keepdims=True))
        vh = vscr[hh, pl.ds(kstart, LAT_KWIN), :]
        o = o + jnp.dot((e_loc * inv).astype(BF16), vh, preferred_element_type=F32)
        o = o + jnp.dot((e_ctx * inv).astype(BF16), cvscr[hh], preferred_element_type=F32)
    o_ref[...] = o.astype(BF16)


def _bias_tables(rel_bias):
    col = jnp.arange(GRID_W)
    col_start = jnp.clip(col - WIN_COLS // 2, 0, GRID_W - WIN_COLS)
    col_mask = (col[None, :] >= col_start[:, None]) & (col[None, :] < col_start[:, None] + WIN_COLS)
    dc_idx = jnp.clip(col[None, :] - col[:, None], -(WIN_COLS - 1), WIN_COLS - 1) + (WIN_COLS - 1)
    t = jnp.where(col_mask, rel_bias.astype(F32)[:, :, :, dc_idx], NEG_INF)
    neg = jnp.full(t.shape[:2] + (1,) + t.shape[3:], NEG_INF, F32)
    tp = jnp.concatenate([neg, t, neg], axis=2)
    return jnp.concatenate([tp[:, :, :2 * WIN_ROWS], tp[:, :, 1:2 * WIN_ROWS + 1]], axis=-1)


def _lat_attention(proj, cache_k, cache_v, t2, qn_g, kn_g, layer):
    hp_w = 2 * HEAD_DIM
    q_col = POOL_WIDTH // hp_w
    k_col = (POOL_WIDTH + ATTN_WIDTH) // hp_w
    v_col = (POOL_WIDTH + 2 * ATTN_WIDTH) // hp_w
    lat_row_seq = M_CTX // DEC_SEQ
    lat_row_blk = M_CTX // LAT_QBLK
    n_blk = DEC_SEQ // LAT_QBLK
    return pl.pallas_call(
        _lat_attn_kernel,
        out_shape=jax.ShapeDtypeStruct((M_LAT, ATTN_WIDTH), BF16),
        grid=(DEC_BATCH, N_HEADS // 2, n_blk),
        in_specs=[
            pl.BlockSpec((LAT_QBLK, hp_w), lambda b, hp, blk: (lat_row_blk + b * n_blk + blk, q_col + hp)),
            pl.BlockSpec((DEC_SEQ, hp_w), lambda b, hp, blk: (lat_row_seq + b, k_col + hp)),
            pl.BlockSpec((DEC_SEQ, hp_w), lambda b, hp, blk: (lat_row_seq + b, v_col + hp)),
            pl.BlockSpec((1, 1, 2, PAST_LEN, HEAD_DIM), lambda b, hp, blk: (b, layer, hp, 0, 0)),
            pl.BlockSpec((1, 1, 2, PAST_LEN, HEAD_DIM), lambda b, hp, blk: (b, layer, hp, 0, 0)),
            pl.BlockSpec((1, 2, 2 * WIN_ROWS, GRID_W, 2 * GRID_W), lambda b, hp, blk: (layer, hp, 0, 0, 0)),
            pl.BlockSpec((1, PAIR_W), lambda b, hp, blk: (0, 0)),
            pl.BlockSpec((1, PAIR_W), lambda b, hp, blk: (0, 0)),
        ],
        out_specs=pl.BlockSpec((LAT_QBLK, hp_w), lambda b, hp, blk: (b * n_blk + blk, hp)),
        scratch_shapes=[
            pltpu.VMEM((DEC_SEQ, PAIR_W), BF16),
            pltpu.VMEM((2, DEC_SEQ, PAIR_W), BF16),
            pltpu.VMEM((PAST_LEN, PAIR_W), BF16),
            pltpu.VMEM((2, PAST_LEN, PAIR_W), BF16),
            pltpu.VMEM((LAT_QBLK, LAT_KWIN), F32),
        ],
        compiler_params=_cparams(("arbitrary", "arbitrary", "arbitrary"), 48 * 2**20),
        name="lat_attention",
    )(proj, proj, proj, cache_k, cache_v, t2, _pair_gain(qn_g), _pair_gain(kn_g))


def _mix_kernel(p_ref, ac_ref, al_ref, wbp_ref, wba_ref, gp_ref, ga_ref, o_ref, sbp, sba):
    m = pl.program_id(1)

    @pl.when(m == 0)
    def _():
        _cast_weight(wbp_ref.at[0], sbp, POOL_WIDTH)
        _cast_weight(wba_ref.at[0], sba, ATTN_WIDTH)

    a = jnp.where(m < M_CTX // TM_MIX, ac_ref[...], al_ref[...])
    yp = jnp.dot(p_ref[...], sbp[...], preferred_element_type=F32)
    ya = jnp.dot(a, sba[...], preferred_element_type=F32)
    o_ref[...] = (_sigmoid(gp_ref[...]) * yp + _sigmoid(ga_ref[...]) * ya).astype(BF16)


def _mix_merge(pool_out, attn_ctx, attn_lat, w_bp, w_ba, proj, layer):
    tm, tn = TM_MIX, TN_MIX
    gp_col = (POOL_WIDTH + 3 * ATTN_WIDTH) // tn
    ga_col = gp_col + D_MODEL // tn
    n_ctx = M_CTX // tm
    return pl.pallas_call(
        _mix_kernel,
        out_shape=jax.ShapeDtypeStruct((M_TOK, D_MODEL), BF16),
        grid=(D_MODEL // tn, M_TOK // tm),
        in_specs=[
            pl.BlockSpec((tm, POOL_WIDTH), lambda n, m: (m, 0)),
            pl.BlockSpec((tm, ATTN_WIDTH), lambda n, m: (jnp.minimum(m, n_ctx - 1), 0)),
            pl.BlockSpec((tm, ATTN_WIDTH), lambda n, m: (jnp.maximum(m - n_ctx, 0), 0)),
            pl.BlockSpec((1, POOL_WIDTH, tn), lambda n, m: (layer, 0, n)),
            pl.BlockSpec((1, ATTN_WIDTH, tn), lambda n, m: (layer, 0, n)),
            pl.BlockSpec((tm, tn), lambda n, m: (m, gp_col + n)),
            pl.BlockSpec((tm, tn), lambda n, m: (m, ga_col + n)),
        ],
        out_specs=pl.BlockSpec((tm, tn), lambda n, m: (m, n)),
        scratch_shapes=[pltpu.VMEM((POOL_WIDTH, tn), BF16), pltpu.VMEM((ATTN_WIDTH, tn), BF16)],
        compiler_params=_cparams(("arbitrary", "arbitrary"), 48 * 2**20),
        name="mix_merge",
    )(pool_out, attn_ctx, attn_lat, w_bp, w_ba, proj, proj)


def _out_proj_kernel(a_ref, w_ref, x_ref, mod_ref, o_ref, wscr, *, row):
    @pl.when(pl.program_id(1) == 0)
    def _():
        _cast_weight(w_ref.at[0], wscr, D_MODEL)

    y = jnp.dot(a_ref[...], wscr[...], preferred_element_type=F32)
    o_ref[...] = x_ref[...] + mod_ref[0, row:row + 1, :] * y


def _out_proj_residual(merged, w_out, layer, x, mods, row):
    tm, tn = TM_OUT, TN_OUT
    return pl.pallas_call(
        functools.partial(_out_proj_kernel, row=row),
        out_shape=jax.ShapeDtypeStruct((M_TOK, D_MODEL), F32),
        grid=(D_MODEL // tn, M_TOK // tm),
        in_specs=[
            pl.BlockSpec((tm, D_MODEL), lambda n, m: (m, 0)),
            pl.BlockSpec((1, D_MODEL, tn), lambda n, m: (layer, 0, n)),
            pl.BlockSpec((tm, tn), lambda n, m: (m, n)),
            pl.BlockSpec((1, MOD_ROWS, tn), lambda n, m: (_group_of_tile(m, tm), 0, n)),
        ],
        out_specs=pl.BlockSpec((tm, tn), lambda n, m: (m, n)),
        scratch_shapes=[pltpu.VMEM((D_MODEL, tn), BF16)],
        compiler_params=_cparams(("arbitrary", "arbitrary"), VMEM_BUDGET),
        name="out_proj",
    )(merged, w_out, x, mods)


def _first_block_of_expert(be_ref, i):
    prev = be_ref[jnp.maximum(i - 1, 0)]
    return jnp.logical_or(i == 0, be_ref[i] != prev)


def _ffn1_kernel(be_ref, nv_ref, a_ref, w1_ref, w3_ref, o_ref, s1, s3):
    i = pl.program_id(1)

    @pl.when(_first_block_of_expert(be_ref, i))
    def _():
        _cast_weight(w1_ref.at[0, 0], s1, D_MODEL)
        _cast_weight(w3_ref.at[0, 0], s3, D_MODEL)

    @pl.when(i < nv_ref[0])
    def _():
        a = a_ref[...]
        h1 = jnp.dot(a, s1[...], preferred_element_type=F32)
        h3 = jnp.dot(a, s3[...], preferred_element_type=F32)
        o_ref[...] = (_silu(h1) * h3).astype(BF16)

    @pl.when(i >= nv_ref[0])
    def _():
        o_ref[...] = jnp.zeros_like(o_ref)


def _ffn_up(a, w1, w3, j, block_expert, n_valid, tm):
    rows = a.shape[0]
    tn = TN_FFN1
    grid_spec = pltpu.PrefetchScalarGridSpec(
        num_scalar_prefetch=2,
        grid=(D_FF // tn, rows // tm),
        in_specs=[
            pl.BlockSpec((tm, D_MODEL), lambda n, i, be, nv: (i, 0)),
            pl.BlockSpec((1, 1, D_MODEL, tn), lambda n, i, be, nv: (j, be[i], 0, n)),
            pl.BlockSpec((1, 1, D_MODEL, tn), lambda n, i, be, nv: (j, be[i], 0, n)),
        ],
        out_specs=pl.BlockSpec((tm, tn), lambda n, i, be, nv: (i, n)),
        scratch_shapes=[pltpu.VMEM((D_MODEL, tn), BF16), pltpu.VMEM((D_MODEL, tn), BF16)],
    )
    return pl.pallas_call(
        _ffn1_kernel,
        out_shape=jax.ShapeDtypeStruct((rows, D_FF), BF16),
        grid_spec=grid_spec,
        compiler_params=_cparams(("arbitrary", "arbitrary"), 48 * 2**20),
        name="ffn_up",
    )(block_expert, n_valid, a, w1, w3)


def _ffn2_dense_kernel(be_ref, nv_ref, a_ref, w_ref, x_ref, mod_ref, o_ref, wscr, *, row):
    i = pl.program_id(1)

    @pl.when(_first_block_of_expert(be_ref, i))
    def _():
        _cast_weight(w_ref.at[0, 0], wscr, D_FF)

    y = jnp.dot(a_ref[...], wscr[...], preferred_element_type=F32)
    o_ref[...] = x_ref[...] + mod_ref[0, row:row + 1, :] * y


def _ffn2_moe_kernel(be_ref, nv_ref, a_ref, w_ref, o_ref, wscr):
    i = pl.program_id(1)

    @pl.when(_first_block_of_expert(be_ref, i))
    def _():
        _cast_weight(w_ref.at[0, 0], wscr, D_FF)

    @pl.when(i < nv_ref[0])
    def _():
        o_ref[...] = jnp.dot(a_ref[...], wscr[...], preferred_element_type=F32)

    @pl.when(i >= nv_ref[0])
    def _():
        o_ref[...] = jnp.zeros_like(o_ref)


def _ffn_down(h, w2, j, block_expert, n_valid, tm, x=None, mods=None, row=None):
    rows = h.shape[0]
    tn = TN_FFN2
    in_specs = [
        pl.BlockSpec((tm, D_FF), lambda n, i, be, nv: (i, 0)),
        pl.BlockSpec((1, 1, D_FF, tn), lambda n, i, be, nv: (j, be[i], 0, n)),
    ]
    args = [h, w2]
    if x is not None:
        kern = functools.partial(_ffn2_dense_kernel, row=row)
        in_specs += [
            pl.BlockSpec((tm, tn), lambda n, i, be, nv: (i, n)),
            pl.BlockSpec((1, MOD_ROWS, tn), lambda n, i, be, nv: (_group_of_tile(i, tm), 0, n)),
        ]
        args += [x, mods]
    else:
        kern = _ffn2_moe_kernel
    grid_spec = pltpu.PrefetchScalarGridSpec(
        num_scalar_prefetch=2,
        grid=(D_MODEL // tn, rows // tm),
        in_specs=in_specs,
        out_specs=pl.BlockSpec((tm, tn), lambda n, i, be, nv: (i, n)),
        scratch_shapes=[pltpu.VMEM((D_FF, tn), BF16)],
    )
    return pl.pallas_call(
        kern,
        out_shape=jax.ShapeDtypeStruct((rows, D_MODEL), F32),
        grid_spec=grid_spec,
        compiler_params=_cparams(("arbitrary", "arbitrary"), VMEM_BUDGET),
        name="ffn_down",
    )(block_expert, n_valid, *args)


def _row_copy(src_hbm, src_row, dst, dst_row, sem):
    return pltpu.make_async_copy(src_hbm.at[pl.ds(src_row, 1), :], dst.at[pl.ds(dst_row, 1), :], sem)


ROW_DMA_UNROLL = 8


def _for_each_row(n_rows, fn):
    def body(c, carry):
        for u in range(ROW_DMA_UNROLL):
            fn(c * ROW_DMA_UNROLL + u, u)
        return carry

    lax.fori_loop(0, n_rows // ROW_DMA_UNROLL, body, 0)


def _gather_kernel(tok_ref, u_hbm, o_ref, buf, sem):
    i = pl.program_id(0)
    nb = pl.num_programs(0)
    tm = o_ref.shape[0]

    def issue(block, slot):
        _for_each_row(tm, lambda r, u: _row_copy(
            u_hbm, tok_ref[block * tm + r], buf.at[slot], r, sem.at[slot]).start(priority=u % 2))

    @pl.when(i == 0)
    def _():
        issue(0, 0)

    @pl.when(i + 1 < nb)
    def _():
        issue(i + 1, (i + 1) % 2)

    slot = i % 2
    _for_each_row(tm, lambda r, u: _row_copy(u_hbm, 0, buf.at[slot], r, sem.at[slot]).wait())
    o_ref[...] = buf[slot].astype(BF16)


def _gather_rows(u, slot_token):
    tm = TM_FFN
    grid_spec = pltpu.PrefetchScalarGridSpec(
        num_scalar_prefetch=1,
        grid=(N_SLOTS // tm,),
        in_specs=[pl.BlockSpec(memory_space=pl.ANY)],
        out_specs=pl.BlockSpec((tm, D_MODEL), lambda i, tok: (i, 0)),
        scratch_shapes=[pltpu.VMEM((2, tm, D_MODEL), F32), pltpu.SemaphoreType.DMA((2,))],
    )
    return pl.pallas_call(
        _gather_kernel,
        out_shape=jax.ShapeDtypeStruct((N_SLOTS, D_MODEL), BF16),
        grid_spec=grid_spec,
        compiler_params=_cparams(("arbitrary",), 32 * 2**20),
        name="moe_gather",
    )(slot_token, u)


def _combine_kernel(d_ref, y_hbm, x_ref, r_ref, mod_ref, o_ref, buf, sem, *, row):
    i = pl.program_id(0)
    nb = pl.num_programs(0)
    tm = o_ref.shape[0]

    def issue(block, slot):
        def start_pair(r, u):
            t = block * tm + r
            _row_copy(y_hbm, d_ref[2 * t], buf.at[slot, 0], r, sem.at[slot]).start(priority=0)
            _row_copy(y_hbm, d_ref[2 * t + 1], buf.at[slot, 1], r, sem.at[slot]).start(priority=1)

        _for_each_row(tm, start_pair)

    @pl.when(i == 0)
    def _():
        issue(0, 0)

    @pl.when(i + 1 < nb)
    def _():
        issue(i + 1, (i + 1) % 2)

    slot = i % 2

    def wait_pair(r, u):
        _row_copy(y_hbm, 0, buf.at[slot, 0], r, sem.at[slot]).wait()
        _row_copy(y_hbm, 0, buf.at[slot, 1], r, sem.at[slot]).wait()

    _for_each_row(tm, wait_pair)
    g1 = r_ref[:, 2:3]
    g2 = r_ref[:, 3:4]
    y = buf[slot, 0] * g1 + buf[slot, 1] * g2
    o_ref[...] = x_ref[...] + mod_ref[0, row:row + 1, :] * y


def _moe_combine(x, y_slots, dest, route, mods, row):
    tm = TM_COMB
    grid_spec = pltpu.PrefetchScalarGridSpec(
        num_scalar_prefetch=1,
        grid=(M_TOK // tm,),
        in_specs=[
            pl.BlockSpec(memory_space=pl.ANY),
            pl.BlockSpec((tm, D_MODEL), lambda i, d: (i, 0)),
            pl.BlockSpec((tm, LANES), lambda i, d: (i, 0)),
            pl.BlockSpec((1, MOD_ROWS, D_MODEL), lambda i, d: (_group_of_tile(i, tm), 0, 0)),
        ],
        out_specs=pl.BlockSpec((tm, D_MODEL), lambda i, d: (i, 0)),
        scratch_shapes=[pltpu.VMEM((2, 2, tm, D_MODEL), F32), pltpu.SemaphoreType.DMA((2,))],
    )
    return pl.pallas_call(
        functools.partial(_combine_kernel, row=row),
        out_shape=jax.ShapeDtypeStruct((M_TOK, D_MODEL), F32),
        grid_spec=grid_spec,
        compiler_params=_cparams(("arbitrary",), 32 * 2**20),
        name="moe_combine",
    )(dest.reshape(-1), y_slots, x, route, mods)


def _moe_plan(route):
    tm = TM_FFN
    expert = route[:, :TOP_K].astype(jnp.int32).reshape(-1)
    onehot = (expert[:, None] == jnp.arange(N_EXPERTS, dtype=jnp.int32)[None, :]).astype(jnp.int32)
    csum = jnp.cumsum(onehot, axis=0)
    rank = jnp.take_along_axis(csum, expert[:, None], axis=1)[:, 0] - 1
    counts = csum[-1]
    padded = (counts + tm - 1) // tm * tm
    pend = jnp.cumsum(padded)
    pstart = pend - padded
    dest = pstart[expert] + rank
    token = jnp.arange(M_TOK * TOP_K, dtype=jnp.int32) // TOP_K
    slot_token = jnp.zeros((N_SLOTS,), jnp.int32).at[dest].set(token)
    block_expert = jnp.minimum(
        jnp.searchsorted(pend, jnp.arange(N_SLOTS_BLOCKS, dtype=jnp.int32) * tm, side='right'),
        N_EXPERTS - 1).astype(jnp.int32)
    n_valid = (pend[-1:] // tm).astype(jnp.int32)
    return slot_token, dest.astype(jnp.int32).reshape(M_TOK, TOP_K), block_expert, n_valid


def kernel(x_prompt, x_sample, cache_k, cache_v, c, c_ctx, ada_w, ada_b, norm1_g, norm2_g, w_in, qn_g, kn_g, pool_w, pool_scale, rel_bias, w_bp, w_ba, w_out, ffn_w1, ffn_w3, ffn_w2, router_w, router_b, moe_w1, moe_w3, moe_w2):
    x = jnp.concatenate([x_prompt.reshape(M_CTX, D_MODEL), x_sample.reshape(M_LAT, D_MODEL)], axis=0)
    cond8 = jnp.zeros((MOD_ROWS, D_MODEL), F32).at[0].set(c_ctx).at[1:1 + DEC_BATCH].set(c)
    mods_all = _adaln_all(cond8, ada_w, ada_b)
    mods_all = mods_all[:, :N_GROUPS].reshape(DEPTH, N_GROUPS, 6, D_MODEL)
    mods_all = jnp.pad(mods_all, ((0, 0), (0, 0), (0, MOD_ROWS - 6), (0, 0)))
    t2_all = _bias_tables(rel_bias)

    up_blocks = jnp.zeros((M_TOK // TM_FFN_UP_DENSE,), jnp.int32)
    up_valid = jnp.full((1,), M_TOK // TM_FFN_UP_DENSE, jnp.int32)
    down_blocks = jnp.zeros((M_TOK // TM_FFN,), jnp.int32)
    down_valid = jnp.full((1,), M_TOK // TM_FFN, jnp.int32)

    new_k = jnp.zeros((BATCH, DEPTH, N_HEADS, SEQ, HEAD_DIM), F32)
    new_v = jnp.zeros((BATCH, DEPTH, N_HEADS, SEQ, HEAD_DIM), F32)
    for l in range(DEPTH):
        j = l // 2
        mods = mods_all[l]
        u = _norm_mod(x, norm1_g[l], mods, 0)
        proj = _in_proj(u, w_in, l)
        pool_out = _pool_mixer(proj, pool_w[l], pool_scale[l])
        attn_ctx, new_k, new_v = _ctx_attention(proj, qn_g[l], kn_g[l], l, new_k, new_v)
        attn_lat = _lat_attention(proj, cache_k, cache_v, t2_all, qn_g[l], kn_g[l], l)
        merged = _mix_merge(pool_out, attn_ctx, attn_lat, w_bp, w_ba, proj, l)
        x = _out_proj_residual(merged, w_out, l, x, mods, 2)
        if l % 2 == 0:
            u2 = _norm_mod(x, norm2_g[l], mods, 3)
            h = _ffn_up(u2, ffn_w1[:, None], ffn_w3[:, None], j, up_blocks, up_valid, TM_FFN_UP_DENSE)
            x = _ffn_down(h, ffn_w2[:, None], j, down_blocks, down_valid, TM_FFN, x=x, mods=mods, row=5)
        else:
            u2, route = _norm_route(x, norm2_g[l], mods, 3, router_w[j], router_b[j])
            slot_token, dest, block_expert, n_valid = _moe_plan(route)
            xs = _gather_rows(u2, slot_token)
            h = _ffn_up(xs, moe_w1, moe_w3, j, block_expert, n_valid, TM_FFN)
            y_slots = _ffn_down(h, moe_w2, j, block_expert, n_valid, TM_FFN)
            x = _moe_combine(x, y_slots, dest, route, mods, 5)

    y_prompt = x[:M_CTX].reshape(BATCH, SEQ, D_MODEL)
    y_sample = x[M_CTX:].reshape(DEC_BATCH, DEC_SEQ, D_MODEL)
    return (y_prompt, y_sample, new_k, new_v)
```

```python
import functools

import jax
import jax.numpy as jnp
from jax import lax
from jax.experimental import pallas as pl
from jax.experimental.pallas import tpu as pltpu

F32 = jnp.float32
BF16 = jnp.bfloat16

D_MODEL = 2048
BATCH = 32
SEQ = 256
DEPTH = 4
DEC_BATCH = 2
DEC_SEQ = 2048
PAST_LEN = 512
GRID_W = 64
GRID_ROWS = DEC_SEQ // GRID_W
POOL_WIDTH = D_MODEL // 2
N_POOL_GROUPS = 4
POOL_GROUP_W = POOL_WIDTH // N_POOL_GROUPS
POOL_WINDOWS = (2, 4, 8, 16)
ATTN_WIDTH = D_MODEL // 2
HEAD_DIM = 64
N_HEADS = ATTN_WIDTH // HEAD_DIM
WIN_ROWS = 8
WIN_COLS = 16
D_FF = ((8 * D_MODEL // 3 + 255) // 256) * 256
N_EXPERTS = 8
TOP_K = 2
PROJ_WIDTH = POOL_WIDTH + 3 * ATTN_WIDTH + 2 * D_MODEL
RMS_EPS = 1e-6
NEG_INF = -1e30
ATTN_SCALE = HEAD_DIM ** -0.5

M_CTX = BATCH * SEQ
M_LAT = DEC_BATCH * DEC_SEQ
M_TOK = M_CTX + M_LAT
N_GROUPS = 1 + DEC_BATCH
MOD_ROWS = 8

LANES = 128
V7X_VMEM_BYTES = 64 * 1024 * 1024
VMEM_BUDGET = V7X_VMEM_BYTES - 8 * 1024 * 1024

CAST_ROWS = 256

TM_NORM = 512
TM_IN, TN_IN = 1024, 1024
TM_MIX, TN_MIX = 512, 1024
TM_OUT, TN_OUT = 1024, 1024
TM_FFN, TN_FFN1, TN_FFN2 = 512, 512, 512
TM_FFN_UP_DENSE = 1024
TM_COMB = 256
POOL_TILE = 256
POOL_HALO = 8
LAT_QROWS = 8
LAT_QBLK = LAT_QROWS * GRID_W
LAT_KROWS = 16
LAT_KWIN = LAT_KROWS * GRID_W
N_SLOTS_BLOCKS = (M_TOK * TOP_K) // TM_FFN + N_EXPERTS
N_SLOTS = N_SLOTS_BLOCKS * TM_FFN


def _cparams(semantics, vmem_bytes):
    return pltpu.CompilerParams(dimension_semantics=semantics, vmem_limit_bytes=int(vmem_bytes))


def _group_of_tile(m, tm):
    start = m * tm
    return jnp.where(start < M_CTX, 0, 1 + (start - M_CTX) // DEC_SEQ)


def _sigmoid(z):
    return 1.0 / (1.0 + jnp.exp(-z))


def _silu(z):
    return z / (1.0 + jnp.exp(-z))


def _rms(x):
    return x * lax.rsqrt(jnp.mean(x * x, axis=-1, keepdims=True) + RMS_EPS)


def _cast_weight(w_ref, scr_ref, k_rows):
    def body(i, carry):
        r = pl.multiple_of(i * CAST_ROWS, CAST_ROWS)
        scr_ref[pl.ds(r, CAST_ROWS), :] = w_ref[pl.ds(r, CAST_ROWS), :].astype(BF16)
        return carry

    lax.fori_loop(0, k_rows // CAST_ROWS, body, 0)


def _ada_kernel(c_ref, w_ref, b_ref, o_ref):
    s = _silu(c_ref[...]).astype(BF16)
    w = w_ref[0].astype(BF16)
    o_ref[0] = jnp.dot(s, w, preferred_element_type=F32) + b_ref[0]


def _adaln_all(cond8, ada_w, ada_b):
    tn = 1024
    n_out = 6 * D_MODEL
    return pl.pallas_call(
        _ada_kernel,
        out_shape=jax.ShapeDtypeStruct((DEPTH, MOD_ROWS, n_out), F32),
        grid=(DEPTH, n_out // tn),
        in_specs=[
            pl.BlockSpec((MOD_ROWS, D_MODEL), lambda l, n: (0, 0)),
            pl.BlockSpec((1, D_MODEL, tn), lambda l, n: (l, 0, n)),
            pl.BlockSpec((1, 1, tn), lambda l, n: (l, 0, n)),
        ],
        out_specs=pl.BlockSpec((1, MOD_ROWS, tn), lambda l, n: (l, 0, n)),
        compiler_params=_cparams(("arbitrary", "arbitrary"), 40 * 2**20),
        name="adaln",
    )(cond8, ada_w, ada_b.reshape(DEPTH, 1, n_out))


def _norm_mod_kernel(x_ref, g_ref, mod_ref, o_ref, *, row):
    y = _rms(x_ref[...]) * g_ref[...]
    sh = mod_ref[0, row:row + 1, :]
    sc = mod_ref[0, row + 1:row + 2, :]
    o_ref[...] = (y * (1.0 + sc) + sh).astype(o_ref.dtype)


def _norm_mod(x, g, mods, row):
    tm = TM_NORM
    return pl.pallas_call(
        functools.partial(_norm_mod_kernel, row=row),
        out_shape=jax.ShapeDtypeStruct((M_TOK, D_MODEL), BF16),
        grid=(M_TOK // tm,),
        in_specs=[
            pl.BlockSpec((tm, D_MODEL), lambda m: (m, 0)),
            pl.BlockSpec((1, D_MODEL), lambda m: (0, 0)),
            pl.BlockSpec((1, MOD_ROWS, D_MODEL), lambda m: (_group_of_tile(m, tm), 0, 0)),
        ],
        out_specs=pl.BlockSpec((tm, D_MODEL), lambda m: (m, 0)),
        compiler_params=_cparams(("parallel",), 32 * 2**20),
        name="norm_mod",
    )(x, g.reshape(1, D_MODEL), mods)


def _norm_route_kernel(x_ref, g_ref, mod_ref, rwh_ref, rwl_ref, rb_ref, u_ref, r_ref, *, row):
    y = _rms(x_ref[...]) * g_ref[...]
    sh = mod_ref[0, row:row + 1, :]
    sc = mod_ref[0, row + 1:row + 2, :]
    u = y * (1.0 + sc) + sh
    u_ref[...] = u.reshape(u_ref.shape)
    u_hi = u.astype(BF16)
    u_lo = (u - u_hi.astype(F32)).astype(BF16)
    logits = (jnp.dot(u_hi, rwh_ref[...], preferred_element_type=F32)
              + jnp.dot(u_hi, rwl_ref[...], preferred_element_type=F32)
              + jnp.dot(u_lo, rwh_ref[...], preferred_element_type=F32)) + rb_ref[...]
    lane = lax.broadcasted_iota(jnp.int32, logits.shape, 1).astype(F32)
    big = float(LANES)
    m1 = jnp.max(logits, axis=-1, keepdims=True)
    i1 = jnp.min(jnp.where(logits == m1, lane, big), axis=-1, keepdims=True)
    rest = jnp.where(lane == i1, -jnp.inf, logits)
    m2 = jnp.max(rest, axis=-1, keepdims=True)
    i2 = jnp.min(jnp.where(rest == m2, lane, big), axis=-1, keepdims=True)
    e2 = jnp.exp(m2 - m1)
    den = 1.0 + e2
    r_ref[...] = jnp.where(lane == 0.0, i1,
                           jnp.where(lane == 1.0, i2,
                                     jnp.where(lane == 2.0, 1.0 / den,
                                               jnp.where(lane == 3.0, e2 / den, 0.0))))


def _norm_route(x, g, mods, row, router_w, router_b):
    tm = TM_NORM
    rw = jnp.zeros((D_MODEL, LANES), F32).at[:, :N_EXPERTS].set(router_w)
    rw_hi = rw.astype(BF16)
    rw_lo = (rw - rw_hi.astype(F32)).astype(BF16)
    rb = jnp.full((1, LANES), -jnp.inf, F32).at[0, :N_EXPERTS].set(router_b)
    return pl.pallas_call(
        functools.partial(_norm_route_kernel, row=row),
        out_shape=(jax.ShapeDtypeStruct((M_TOK, 1, D_MODEL), F32),
                   jax.ShapeDtypeStruct((M_TOK, LANES), F32)),
        grid=(M_TOK // tm,),
        in_specs=[
            pl.BlockSpec((tm, D_MODEL), lambda m: (m, 0)),
            pl.BlockSpec((1, D_MODEL), lambda m: (0, 0)),
            pl.BlockSpec((1, MOD_ROWS, D_MODEL), lambda m: (_group_of_tile(m, tm), 0, 0)),
            pl.BlockSpec((D_MODEL, LANES), lambda m: (0, 0)),
            pl.BlockSpec((D_MODEL, LANES), lambda m: (0, 0)),
            pl.BlockSpec((1, LANES), lambda m: (0, 0)),
        ],
        out_specs=(pl.BlockSpec((tm, 1, D_MODEL), lambda m: (m, 0, 0)),
                   pl.BlockSpec((tm, LANES), lambda m: (m, 0))),
        compiler_params=_cparams(("parallel",), 40 * 2**20),
        name="norm_route",
    )(x, g.reshape(1, D_MODEL), mods, rw_hi, rw_lo, rb)


def _in_proj_kernel(a_ref, w_ref, o_ref, wscr):
    @pl.when(pl.program_id(1) == 0)
    def _():
        _cast_weight(w_ref.at[0], wscr, D_MODEL)

    o_ref[...] = jnp.dot(a_ref[...], wscr[...], preferred_element_type=F32)


def _in_proj(u, w_in, layer):
    tm, tn = TM_IN, TN_IN
    return pl.pallas_call(
        _in_proj_kernel,
        out_shape=jax.ShapeDtypeStruct((M_TOK, PROJ_WIDTH), F32),
        grid=(PROJ_WIDTH // tn, M_TOK // tm),
        in_specs=[
            pl.BlockSpec((tm, D_MODEL), lambda n, m: (m, 0)),
            pl.BlockSpec((1, D_MODEL, tn), lambda n, m: (layer, 0, n)),
        ],
        out_specs=pl.BlockSpec((tm, tn), lambda n, m: (m, n)),
        scratch_shapes=[pltpu.VMEM((D_MODEL, tn), BF16)],
        compiler_params=_cparams(("arbitrary", "arbitrary"), VMEM_BUDGET),
        name="in_proj",
    )(u, w_in)


def _pool_kernel(xm_ref, xp_ref, xn_ref, pw_ref, ps_ref, o_ref, pad):
    i = pl.program_id(0)
    n_ctx_tiles = M_CTX // POOL_TILE
    tiles_per_lat = DEC_SEQ // POOL_TILE
    is_lat = i >= n_ctx_tiles
    j = (i - n_ctx_tiles) % tiles_per_lat
    has_prev = jnp.logical_and(is_lat, j != 0)
    has_next = jnp.logical_and(is_lat, j != tiles_per_lat - 1)
    pad[0:POOL_HALO, :] = jnp.where(has_prev, xp_ref[...], 0.0)
    pad[POOL_HALO:POOL_HALO + POOL_TILE, :] = xm_ref[...]
    pad[POOL_HALO + POOL_TILE:, :] = jnp.where(has_next, xn_ref[...], 0.0)
    off = jnp.where(is_lat, j * POOL_TILE, 0)
    seq_len = jnp.where(is_lat, DEC_SEQ, SEQ)
    pos = lax.broadcasted_iota(jnp.int32, (POOL_TILE, 1), 0) + off
    for g, win in enumerate(POOL_WINDOWS):
        lanes = slice(g * POOL_GROUP_W, (g + 1) * POOL_GROUP_W)
        back, fwd = win // 2, win - win // 2
        acc = pad[pl.ds(POOL_HALO - back, POOL_TILE), lanes]
        for d in range(-back + 1, fwd):
            acc = acc + pad[pl.ds(POOL_HALO + d, POOL_TILE), lanes]
        cnt = (jnp.minimum(pos + fwd, seq_len) - jnp.maximum(pos - back, 0)).astype(F32)
        pooled = (acc / cnt - xm_ref[:, lanes]).astype(BF16)
        mixed = jnp.dot(pooled, pw_ref[g].astype(BF16), preferred_element_type=F32)
        o_ref[:, lanes] = (mixed * ps_ref[:, lanes]).astype(BF16)


def _pool_mixer(proj, pool_w, pool_scale):
    t, h = POOL_TILE, POOL_HALO
    hb = t // h
    last_halo = M_TOK // h - 1
    return pl.pallas_call(
        _pool_kernel,
        out_shape=jax.ShapeDtypeStruct((M_TOK, POOL_WIDTH), BF16),
        grid=(M_TOK // t,),
        in_specs=[
            pl.BlockSpec((t, POOL_WIDTH), lambda i: (i, 0)),
            pl.BlockSpec((h, POOL_WIDTH), lambda i: (jnp.maximum(i * hb - 1, 0), 0)),
            pl.BlockSpec((h, POOL_WIDTH), lambda i: (jnp.minimum((i + 1) * hb, last_halo), 0)),
            pl.BlockSpec((N_POOL_GROUPS, POOL_GROUP_W, POOL_GROUP_W), lambda i: (0, 0, 0)),
            pl.BlockSpec((1, POOL_WIDTH), lambda i: (0, 0)),
        ],
        out_specs=pl.BlockSpec((t, POOL_WIDTH), lambda i: (i, 0)),
        scratch_shapes=[pltpu.VMEM((t + 2 * h, POOL_WIDTH), F32)],
        compiler_params=_cparams(("parallel",), 24 * 2**20),
        name="pool_mixer",
    )(proj, proj, proj, pool_w, pool_scale.reshape(1, POOL_WIDTH))


PAIR_W = 2 * HEAD_DIM


def _pair_group_ones():
    r = lax.broadcasted_iota(jnp.int32, (PAIR_W, PAIR_W), 0) // HEAD_DIM
    c = lax.broadcasted_iota(jnp.int32, (PAIR_W, PAIR_W), 1) // HEAD_DIM
    return jnp.where(r == c, 1.0, 0.0).astype(BF16)


def _head_masks():
    lane = lax.broadcasted_iota(jnp.int32, (1, PAIR_W), 1)
    return (lane < HEAD_DIM, lane >= HEAD_DIM)


def _rms_pair(x, ones):
    y2 = x * x
    hi = y2.astype(BF16)
    lo = (y2 - hi.astype(F32)).astype(BF16)
    ss = jnp.dot(hi, ones, preferred_element_type=F32) + jnp.dot(lo, ones, preferred_element_type=F32)
    return x * lax.rsqrt(ss * (1.0 / HEAD_DIM) + RMS_EPS)


def _softmax_rows(s):
    e = jnp.exp(s - jnp.max(s, axis=-1, keepdims=True))
    return e * (1.0 / jnp.sum(e, axis=-1, keepdims=True))


_NT_DIMS = (((1,), (1,)), ((), ()))


def _ctx_attn_kernel(q_ref, k_ref, v_ref, qg_ref, kg_ref, _ck_in, _cv_in, o_ref, ko_ref, vo_ref):
    qg = qg_ref[...] * ATTN_SCALE
    kg = kg_ref[...]
    ones = _pair_group_ones()
    masks = _head_masks()
    for p in range(N_HEADS // 2):
        sl = slice(p * PAIR_W, (p + 1) * PAIR_W)
        kn = _rms_pair(k_ref[:, sl], ones) * kg
        v = v_ref[:, sl]
        for hh in range(2):
            lanes = slice(hh * HEAD_DIM, (hh + 1) * HEAD_DIM)
            ko_ref[0, 0, 2 * p + hh] = kn[:, lanes]
            vo_ref[0, 0, 2 * p + hh] = v[:, lanes]
        qn = _rms_pair(q_ref[:, sl], ones) * qg
        kb = kn.astype(BF16)
        o = jnp.zeros((SEQ, PAIR_W), F32)
        for hh in range(2):
            qh = jnp.where(masks[hh], qn, 0.0).astype(BF16)
            s = lax.dot_general(qh, kb, _NT_DIMS, preferred_element_type=F32)
            pr = _softmax_rows(s).astype(BF16)
            vh = jnp.where(masks[hh], v, 0.0).astype(BF16)
            o = o + jnp.dot(pr, vh, preferred_element_type=F32)
        o_ref[:, sl] = o.astype(BF16)


def _pair_gain(g):
    return jnp.concatenate([g, g]).reshape(1, PAIR_W)


def _ctx_attention(proj, qn_g, kn_g, layer, new_k, new_v):
    q_blk = POOL_WIDTH // ATTN_WIDTH
    cache_shape = jax.ShapeDtypeStruct((BATCH, DEPTH, N_HEADS, SEQ, HEAD_DIM), F32)
    cache_spec = pl.BlockSpec((1, 1, N_HEADS, SEQ, HEAD_DIM), lambda b: (b, layer, 0, 0, 0))
    return pl.pallas_call(
        _ctx_attn_kernel,
        out_shape=(jax.ShapeDtypeStruct((M_CTX, ATTN_WIDTH), BF16), cache_shape, cache_shape),
        grid=(BATCH,),
        in_specs=[
            pl.BlockSpec((SEQ, ATTN_WIDTH), lambda b: (b, q_blk)),
            pl.BlockSpec((SEQ, ATTN_WIDTH), lambda b: (b, q_blk + 1)),
            pl.BlockSpec((SEQ, ATTN_WIDTH), lambda b: (b, q_blk + 2)),
            pl.BlockSpec((1, PAIR_W), lambda b: (0, 0)),
            pl.BlockSpec((1, PAIR_W), lambda b: (0, 0)),
            pl.BlockSpec(memory_space=pl.ANY),
            pl.BlockSpec(memory_space=pl.ANY),
        ],
        out_specs=(pl.BlockSpec((SEQ, ATTN_WIDTH), lambda b: (b, 0)), cache_spec, cache_spec),
        input_output_aliases={5: 1, 6: 2},
        compiler_params=_cparams(("arbitrary",), 40 * 2**20),
        name="ctx_attention",
    )(proj, proj, proj, _pair_gain(qn_g), _pair_gain(kn_g), new_k, new_v)


def _lat_attn_kernel(q_ref, k_ref, v_ref, ck_ref, cv_ref, t2_ref, qg_ref, kg_ref, o_ref,
                     kscr, vscr, ckscr, cvscr, sscr):
    blk = pl.program_id(2)

    ones = _pair_group_ones()
    masks = _head_masks()

    @pl.when(blk == 0)
    def _():
        kscr[...] = (_rms_pair(k_ref[...], ones) * kg_ref[...]).astype(BF16)
        ckscr[...] = jnp.concatenate([ck_ref[0, 0, 0], ck_ref[0, 0, 1]], axis=-1).astype(BF16)
        v = v_ref[...]
        cv = jnp.concatenate([cv_ref[0, 0, 0], cv_ref[0, 0, 1]], axis=-1)
        for hh in range(2):
            vscr[hh] = jnp.where(masks[hh], v, 0.0).astype(BF16)
            cvscr[hh] = jnp.where(masks[hh], cv, 0.0).astype(BF16)

    r0 = blk * LAT_QROWS
    w0 = jnp.clip(r0 - WIN_ROWS // 2, 0, GRID_ROWS - LAT_KROWS)
    kstart = pl.multiple_of(w0 * GRID_W, 256)
    lane_lo = lax.broadcasted_iota(jnp.int32, (GRID_W, 2 * GRID_W), 1) < GRID_W
    qn = _rms_pair(q_ref[...], ones) * (qg_ref[...] * ATTN_SCALE)
    kw = kscr[pl.ds(kstart, LAT_KWIN), :]
    ckb = ckscr[...]
    o = jnp.zeros((LAT_QBLK, PAIR_W), F32)
    for hh in range(2):
        qh = jnp.where(masks[hh], qn, 0.0).astype(BF16)
        s = lax.dot_general(qh, kw, _NT_DIMS, preferred_element_type=F32)
        for qi in range(LAT_QROWS):
            qr = r0 + qi
            rs = jnp.clip(qr - WIN_ROWS // 2, 0, GRID_ROWS - WIN_ROWS)
            for jp in range(LAT_KROWS // 2):
                kr0 = w0 + 2 * jp
                idx = jnp.clip(kr0 - qr + WIN_ROWS, 0, 2 * WIN_ROWS - 1)
                ok0 = jnp.logical_and(kr0 >= rs, kr0 < rs + WIN_ROWS).astype(jnp.int32)
                ok1 = jnp.logical_and(kr0 + 1 >= rs, kr0 + 1 < rs + WIN_ROWS).astype(jnp.int32)
                ok = jnp.where(lane_lo, ok0, ok1) > 0
                tile = jnp.where(ok, t2_ref[0, hh, idx], NEG_INF)
                rows = slice(qi * GRID_W, (qi + 1) * GRID_W)
                cols = slice(jp * 2 * GRID_W, (jp + 1) * 2 * GRID_W)
                sscr[rows, cols] = s[rows, cols] + tile
        s_loc = sscr[...]
        s_ctx = lax.dot_general(qh, ckb, _NT_DIMS, preferred_element_type=F32)
        m = jnp.maximum(jnp.max(s_loc, axis=-1, keepdims=True), jnp.max(s_ctx, axis=-1, keepdims=True))
        e_loc = jnp.exp(s_loc - m)
        e_ctx = jnp.exp(s_ctx - m)
        inv = 1.0 / (jnp.sum(e_loc, axis=-1, keepdims=True) + jnp.sum(e_ctx, axis=-1, keepdims=True))
        vh = vscr[hh, pl.ds(kstart, LAT_KWIN), :]
        o = o + jnp.dot((e_loc * inv).astype(BF16), vh, preferred_element_type=F32)
        o = o + jnp.dot((e_ctx * inv).astype(BF16), cvscr[hh], preferred_element_type=F32)
    o_ref[...] = o.astype(BF16)


def _bias_tables(rel_bias):
    col = jnp.arange(GRID_W)
    col_start = jnp.clip(col - WIN_COLS // 2, 0, GRID_W - WIN_COLS)
    col_mask = (col[None, :] >= col_start[:, None]) & (col[None, :] < col_start[:, None] + WIN_COLS)
    dc_idx = jnp.clip(col[None, :] - col[:, None], -(WIN_COLS - 1), WIN_COLS - 1) + (WIN_COLS - 1)
    onehot = (dc_idx[:, :, None] == jnp.arange(2 * WIN_COLS - 1)[None, None, :]).astype(F32)
    t = jnp.einsum('lhrc,qkc->lhrqk', rel_bias.astype(F32), onehot, precision=lax.Precision.HIGHEST)
    t = jnp.where(col_mask, t, NEG_INF)
    neg = jnp.full(t.shape[:2] + (1,) + t.shape[3:], NEG_INF, F32)
    tp = jnp.concatenate([neg, t, neg], axis=2)
    return jnp.concatenate([tp[:, :, :2 * WIN_ROWS], tp[:, :, 1:2 * WIN_ROWS + 1]], axis=-1)


def _lat_attention(proj, cache_k, cache_v, t2, qn_g, kn_g, layer):
    hp_w = 2 * HEAD_DIM
    q_col = POOL_WIDTH // hp_w
    k_col = (POOL_WIDTH + ATTN_WIDTH) // hp_w
    v_col = (POOL_WIDTH + 2 * ATTN_WIDTH) // hp_w
    lat_row_seq = M_CTX // DEC_SEQ
    lat_row_blk = M_CTX // LAT_QBLK
    n_blk = DEC_SEQ // LAT_QBLK
    return pl.pallas_call(
        _lat_attn_kernel,
        out_shape=jax.ShapeDtypeStruct((M_LAT, ATTN_WIDTH), BF16),
        grid=(DEC_BATCH, N_HEADS // 2, n_blk),
        in_specs=[
            pl.BlockSpec((LAT_QBLK, hp_w), lambda b, hp, blk: (lat_row_blk + b * n_blk + blk, q_col + hp)),
            pl.BlockSpec((DEC_SEQ, hp_w), lambda b, hp, blk: (lat_row_seq + b, k_col + hp)),
            pl.BlockSpec((DEC_SEQ, hp_w), lambda b, hp, blk: (lat_row_seq + b, v_col + hp)),
            pl.BlockSpec((1, 1, 2, PAST_LEN, HEAD_DIM), lambda b, hp, blk: (b, layer, hp, 0, 0)),
            pl.BlockSpec((1, 1, 2, PAST_LEN, HEAD_DIM), lambda b, hp, blk: (b, layer, hp, 0, 0)),
            pl.BlockSpec((1, 2, 2 * WIN_ROWS, GRID_W, 2 * GRID_W), lambda b, hp, blk: (layer, hp, 0, 0, 0)),
            pl.BlockSpec((1, PAIR_W), lambda b, hp, blk: (0, 0)),
            pl.BlockSpec((1, PAIR_W), lambda b, hp, blk: (0, 0)),
        ],
        out_specs=pl.BlockSpec((LAT_QBLK, hp_w), lambda b, hp, blk: (b * n_blk + blk, hp)),
        scratch_shapes=[
            pltpu.VMEM((DEC_SEQ, PAIR_W), BF16),
            pltpu.VMEM((2, DEC_SEQ, PAIR_W), BF16),
            pltpu.VMEM((PAST_LEN, PAIR_W), BF16),
            pltpu.VMEM((2, PAST_LEN, PAIR_W), BF16),
            pltpu.VMEM((LAT_QBLK, LAT_KWIN), F32),
        ],
        compiler_params=_cparams(("arbitrary", "arbitrary", "arbitrary"), 48 * 2**20),
        name="lat_attention",
    )(proj, proj, proj, cache_k, cache_v, t2, _pair_gain(qn_g), _pair_gain(kn_g))


def _mix_kernel(p_ref, ac_ref, al_ref, wbp_ref, wba_ref, gp_ref, ga_ref, o_ref, sbp, sba):
    m = pl.program_id(1)

    @pl.when(m == 0)
    def _():
        _cast_weight(wbp_ref.at[0], sbp, POOL_WIDTH)
        _cast_weight(wba_ref.at[0], sba, ATTN_WIDTH)

    a = jnp.where(m < M_CTX // TM_MIX, ac_ref[...], al_ref[...])
    yp = jnp.dot(p_ref[...], sbp[...], preferred_element_type=F32)
    ya = jnp.dot(a, sba[...], preferred_element_type=F32)
    o_ref[...] = (_sigmoid(gp_ref[...]) * yp + _sigmoid(ga_ref[...]) * ya).astype(BF16)


def _mix_merge(pool_out, attn_ctx, attn_lat, w_bp, w_ba, proj, layer):
    tm, tn = TM_MIX, TN_MIX
    gp_col = (POOL_WIDTH + 3 * ATTN_WIDTH) // tn
    ga_col = gp_col + D_MODEL // tn
    n_ctx = M_CTX // tm
    return pl.pallas_call(
        _mix_kernel,
        out_shape=jax.ShapeDtypeStruct((M_TOK, D_MODEL), BF16),
        grid=(D_MODEL // tn, M_TOK // tm),
        in_specs=[
            pl.BlockSpec((tm, POOL_WIDTH), lambda n, m: (m, 0)),
            pl.BlockSpec((tm, ATTN_WIDTH), lambda n, m: (jnp.minimum(m, n_ctx - 1), 0)),
            pl.BlockSpec((tm, ATTN_WIDTH), lambda n, m: (jnp.maximum(m - n_ctx, 0), 0)),
            pl.BlockSpec((1, POOL_WIDTH, tn), lambda n, m: (layer, 0, n)),
            pl.BlockSpec((1, ATTN_WIDTH, tn), lambda n, m: (layer, 0, n)),
            pl.BlockSpec((tm, tn), lambda n, m: (m, gp_col + n)),
            pl.BlockSpec((tm, tn), lambda n, m: (m, ga_col + n)),
        ],
        out_specs=pl.BlockSpec((tm, tn), lambda n, m: (m, n)),
        scratch_shapes=[pltpu.VMEM((POOL_WIDTH, tn), BF16), pltpu.VMEM((ATTN_WIDTH, tn), BF16)],
        compiler_params=_cparams(("arbitrary", "arbitrary"), 48 * 2**20),
        name="mix_merge",
    )(pool_out, attn_ctx, attn_lat, w_bp, w_ba, proj, proj)


def _out_proj_kernel(a_ref, w_ref, x_ref, mod_ref, o_ref, wscr, *, row):
    @pl.when(pl.program_id(1) == 0)
    def _():
        _cast_weight(w_ref.at[0], wscr, D_MODEL)

    y = jnp.dot(a_ref[...], wscr[...], preferred_element_type=F32)
    o_ref[...] = x_ref[...] + mod_ref[0, row:row + 1, :] * y


def _out_proj_residual(merged, w_out, layer, x, mods, row):
    tm, tn = TM_OUT, TN_OUT
    return pl.pallas_call(
        functools.partial(_out_proj_kernel, row=row),
        out_shape=jax.ShapeDtypeStruct((M_TOK, D_MODEL), F32),
        grid=(D_MODEL // tn, M_TOK // tm),
        in_specs=[
            pl.BlockSpec((tm, D_MODEL), lambda n, m: (m, 0)),
            pl.BlockSpec((1, D_MODEL, tn), lambda n, m: (layer, 0, n)),
            pl.BlockSpec((tm, tn), lambda n, m: (m, n)),
            pl.BlockSpec((1, MOD_ROWS, tn), lambda n, m: (_group_of_tile(m, tm), 0, n)),
        ],
        out_specs=pl.BlockSpec((tm, tn), lambda n, m: (m, n)),
        scratch_shapes=[pltpu.VMEM((D_MODEL, tn), BF16)],
        compiler_params=_cparams(("arbitrary", "arbitrary"), VMEM_BUDGET),
        name="out_proj",
    )(merged, w_out, x, mods)


def _ffn_up_kernel(a_ref, w1_ref, w3_ref, o_ref, s1, s3):
    @pl.when(pl.program_id(1) == 0)
    def _():
        _cast_weight(w1_ref.at[0], s1, D_MODEL)
        _cast_weight(w3_ref.at[0], s3, D_MODEL)

    a = a_ref[...]
    h1 = jnp.dot(a, s1[...], preferred_element_type=F32)
    h3 = jnp.dot(a, s3[...], preferred_element_type=F32)
    o_ref[...] = (_silu(h1) * h3).astype(BF16)


def _ffn_up(a, w1, w3, j):
    tm, tn = TM_FFN_UP_DENSE, TN_FFN1
    return pl.pallas_call(
        _ffn_up_kernel,
        out_shape=jax.ShapeDtypeStruct((M_TOK, D_FF), BF16),
        grid=(D_FF // tn, M_TOK // tm),
        in_specs=[
            pl.BlockSpec((tm, D_MODEL), lambda n, m: (m, 0)),
            pl.BlockSpec((1, D_MODEL, tn), lambda n, m: (j, 0, n)),
            pl.BlockSpec((1, D_MODEL, tn), lambda n, m: (j, 0, n)),
        ],
        out_specs=pl.BlockSpec((tm, tn), lambda n, m: (m, n)),
        scratch_shapes=[pltpu.VMEM((D_MODEL, tn), BF16), pltpu.VMEM((D_MODEL, tn), BF16)],
        compiler_params=_cparams(("arbitrary", "arbitrary"), 48 * 2**20),
        name="ffn_up",
    )(a, w1, w3)


def _ffn_down_kernel(a_ref, w_ref, x_ref, mod_ref, o_ref, wscr, *, row):
    @pl.when(pl.program_id(1) == 0)
    def _():
        _cast_weight(w_ref.at[0], wscr, D_FF)

    y = jnp.dot(a_ref[...], wscr[...], preferred_element_type=F32)
    o_ref[...] = x_ref[...] + mod_ref[0, row:row + 1, :] * y


def _ffn_down(h, w2, j, x, mods, row):
    tm, tn = TM_FFN, TN_FFN2
    return pl.pallas_call(
        functools.partial(_ffn_down_kernel, row=row),
        out_shape=jax.ShapeDtypeStruct((M_TOK, D_MODEL), F32),
        grid=(D_MODEL // tn, M_TOK // tm),
        in_specs=[
            pl.BlockSpec((tm, D_FF), lambda n, m: (m, 0)),
            pl.BlockSpec((1, D_FF, tn), lambda n, m: (j, 0, n)),
            pl.BlockSpec((tm, tn), lambda n, m: (m, n)),
            pl.BlockSpec((1, MOD_ROWS, tn), lambda n, m: (_group_of_tile(m, tm), 0, n)),
        ],
        out_specs=pl.BlockSpec((tm, tn), lambda n, m: (m, n)),
        scratch_shapes=[pltpu.VMEM((D_FF, tn), BF16)],
        compiler_params=_cparams(("arbitrary", "arbitrary"), VMEM_BUDGET),
        name="ffn_down",
    )(h, w2, x, mods)


N_RUNS = N_EXPERTS + 1


def _run_blocks_pipelined(nb, a_copy, o_copy, compute):
    def body(k, carry):
        slot = k % 2

        @pl.when(k + 1 < nb)
        def _():
            a_copy(k + 1, 1 - slot).start()

        a_copy(k, slot).wait()

        @pl.when(k >= 2)
        def _():
            o_copy(k - 2, slot).wait()

        compute(slot)
        o_copy(k, slot).start()
        return carry

    lax.fori_loop(0, nb, body, 0)

    @pl.when(nb >= 2)
    def _():
        o_copy(nb - 2, nb % 2).wait()

    @pl.when(nb >= 1)
    def _():
        o_copy(nb - 1, (nb - 1) % 2).wait()


def _zero_tail_blocks(nb, obuf, o_copy):
    obuf[0] = jnp.zeros(obuf.shape[1:], obuf.dtype)

    def body(k, carry):
        o_copy(k, 0).start()
        o_copy(k, 0).wait()
        return carry

    lax.fori_loop(0, nb, body, 0)


def _moe_up_kernel(bs_ref, nb_ref, a_hbm, w1_ref, w3_ref, h_hbm, s1, s3, abuf, obuf, sem_a, sem_o):
    n = pl.program_id(0)
    r = pl.program_id(1)
    b0 = bs_ref[r]
    nb = nb_ref[r]
    tm, tn = obuf.shape[1], obuf.shape[2]
    col = pl.multiple_of(n * tn, tn)

    def a_copy(k, slot):
        row = pl.multiple_of((b0 + k) * tm, tm)
        return pltpu.make_async_copy(a_hbm.at[pl.ds(row, tm)], abuf.at[slot], sem_a.at[slot])

    def o_copy(k, slot):
        row = pl.multiple_of((b0 + k) * tm, tm)
        return pltpu.make_async_copy(obuf.at[slot], h_hbm.at[pl.ds(row, tm), pl.ds(col, tn)], sem_o.at[slot])

    def compute(slot):
        a = abuf[slot]
        h1 = jnp.dot(a, s1[...], preferred_element_type=F32)
        h3 = jnp.dot(a, s3[...], preferred_element_type=F32)
        obuf[slot] = (_silu(h1) * h3).astype(BF16)

    @pl.when(jnp.logical_and(r < N_EXPERTS, nb > 0))
    def _():
        a_copy(0, 0).start()
        _cast_weight(w1_ref.at[0, 0], s1, D_MODEL)
        _cast_weight(w3_ref.at[0, 0], s3, D_MODEL)
        _run_blocks_pipelined(nb, a_copy, o_copy, compute)

    @pl.when(r == N_EXPERTS)
    def _():
        _zero_tail_blocks(nb, obuf, o_copy)


def _moe_up(xs, w1, w3, j, run_start, run_blocks):
    tm, tn = TM_FFN, TN_FFN1
    w_spec = pl.BlockSpec((1, 1, D_MODEL, tn), lambda n, r, bs, nb: (j, jnp.minimum(r, N_EXPERTS - 1), 0, n))
    grid_spec = pltpu.PrefetchScalarGridSpec(
        num_scalar_prefetch=2,
        grid=(D_FF // tn, N_RUNS),
        in_specs=[pl.BlockSpec(memory_space=pl.ANY), w_spec, w_spec],
        out_specs=pl.BlockSpec(memory_space=pl.ANY),
        scratch_shapes=[
            pltpu.VMEM((D_MODEL, tn), BF16), pltpu.VMEM((D_MODEL, tn), BF16),
            pltpu.VMEM((2, tm, D_MODEL), BF16), pltpu.VMEM((2, tm, tn), BF16),
            pltpu.SemaphoreType.DMA((2,)), pltpu.SemaphoreType.DMA((2,)),
        ],
    )
    return pl.pallas_call(
        _moe_up_kernel,
        out_shape=jax.ShapeDtypeStruct((N_SLOTS, D_FF), BF16),
        grid_spec=grid_spec,
        compiler_params=_cparams(("arbitrary", "arbitrary"), 40 * 2**20),
        name="moe_up",
    )(run_start, run_blocks, xs, w1, w3)


def _moe_down_kernel(bs_ref, nb_ref, h_hbm, w_ref, y_hbm, wscr, abuf, obuf, sem_a, sem_o):
    n = pl.program_id(0)
    r = pl.program_id(1)
    b0 = bs_ref[r]
    nb = nb_ref[r]
    tm, tn = obuf.shape[1], obuf.shape[3]
    col = pl.multiple_of(n * tn, tn)

    def a_copy(k, slot):
        row = pl.multiple_of((b0 + k) * tm, tm)
        return pltpu.make_async_copy(h_hbm.at[pl.ds(row, tm)], abuf.at[slot], sem_a.at[slot])

    def o_copy(k, slot):
        row = pl.multiple_of((b0 + k) * tm, tm)
        return pltpu.make_async_copy(obuf.at[slot], y_hbm.at[pl.ds(row, tm), :, pl.ds(col, tn)], sem_o.at[slot])

    def compute(slot):
        y = jnp.dot(abuf[slot], wscr[...], preferred_element_type=F32)
        obuf[slot] = y.reshape(obuf.shape[1:])

    @pl.when(jnp.logical_and(r < N_EXPERTS, nb > 0))
    def _():
        a_copy(0, 0).start()
        _cast_weight(w_ref.at[0, 0], wscr, D_FF)
        _run_blocks_pipelined(nb, a_copy, o_copy, compute)

    @pl.when(r == N_EXPERTS)
    def _():
        _zero_tail_blocks(nb, obuf, o_copy)


def _moe_down(h, w2, j, run_start, run_blocks):
    tm, tn = TM_FFN, TN_FFN2
    grid_spec = pltpu.PrefetchScalarGridSpec(
        num_scalar_prefetch=2,
        grid=(D_MODEL // tn, N_RUNS),
        in_specs=[
            pl.BlockSpec(memory_space=pl.ANY),
            pl.BlockSpec((1, 1, D_FF, tn), lambda n, r, bs, nb: (j, jnp.minimum(r, N_EXPERTS - 1), 0, n)),
        ],
        out_specs=pl.BlockSpec(memory_space=pl.ANY),
        scratch_shapes=[
            pltpu.VMEM((D_FF, tn), BF16),
            pltpu.VMEM((2, tm, D_FF), BF16), pltpu.VMEM((2, tm, 1, tn), F32),
            pltpu.SemaphoreType.DMA((2,)), pltpu.SemaphoreType.DMA((2,)),
        ],
    )
    return pl.pallas_call(
        _moe_down_kernel,
        out_shape=jax.ShapeDtypeStruct((N_SLOTS, 1, D_MODEL), F32),
        grid_spec=grid_spec,
        compiler_params=_cparams(("arbitrary", "arbitrary"), VMEM_BUDGET),
        name="moe_down",
    )(run_start, run_blocks, h, w2)


def _row_copy(src_hbm, src_row, dst, dst_row, sem):
    return pltpu.make_async_copy(src_hbm.at[pl.ds(src_row, 1)], dst.at[pl.ds(dst_row, 1)], sem)


ROW_DMA_UNROLL = 8


def _for_each_row(n_rows, fn):
    def body(c, carry):
        for u in range(ROW_DMA_UNROLL):
            fn(c * ROW_DMA_UNROLL + u, u)
        return carry

    lax.fori_loop(0, n_rows // ROW_DMA_UNROLL, body, 0)


def _gather_kernel(tok_ref, u_hbm, o_ref, buf, flat, sem):
    i = pl.program_id(0)
    nb = pl.num_programs(0)
    tm = o_ref.shape[0]

    def issue(block, slot):
        _for_each_row(tm, lambda r, u: _row_copy(
            u_hbm, tok_ref[block * tm + r], buf.at[slot], r, sem.at[slot]).start(priority=u % 2))

    @pl.when(i == 0)
    def _():
        issue(0, 0)

    @pl.when(i + 1 < nb)
    def _():
        issue(i + 1, (i + 1) % 2)

    slot = i % 2
    _for_each_row(tm, lambda r, u: _row_copy(u_hbm, 0, buf.at[slot], r, sem.at[slot]).wait())
    flat[...] = buf[slot].reshape(flat.shape)
    o_ref[...] = flat[...].astype(BF16)


def _gather_rows(u, slot_token):
    tm = TM_FFN
    grid_spec = pltpu.PrefetchScalarGridSpec(
        num_scalar_prefetch=1,
        grid=(N_SLOTS // tm,),
        in_specs=[pl.BlockSpec(memory_space=pl.ANY)],
        out_specs=pl.BlockSpec((tm, D_MODEL), lambda i, tok: (i, 0)),
        scratch_shapes=[pltpu.VMEM((2, tm, 1, D_MODEL), F32), pltpu.VMEM((tm, D_MODEL), F32),
                        pltpu.SemaphoreType.DMA((2,))],
    )
    return pl.pallas_call(
        _gather_kernel,
        out_shape=jax.ShapeDtypeStruct((N_SLOTS, D_MODEL), BF16),
        grid_spec=grid_spec,
        compiler_params=_cparams(("arbitrary",), 32 * 2**20),
        name="moe_gather",
    )(slot_token, u)


def _combine_kernel(d_ref, y_hbm, x_ref, r_ref, mod_ref, o_ref, buf, flat, sem, *, row):
    i = pl.program_id(0)
    nb = pl.num_programs(0)
    tm = o_ref.shape[0]

    def issue(block, slot):
        def start_pair(r, u):
            t = block * tm + r
            _row_copy(y_hbm, d_ref[2 * t], buf.at[slot, 0], r, sem.at[slot]).start(priority=0)
            _row_copy(y_hbm, d_ref[2 * t + 1], buf.at[slot, 1], r, sem.at[slot]).start(priority=1)

        _for_each_row(tm, start_pair)

    @pl.when(i == 0)
    def _():
        issue(0, 0)

    @pl.when(i + 1 < nb)
    def _():
        issue(i + 1, (i + 1) % 2)

    slot = i % 2

    def wait_pair(r, u):
        _row_copy(y_hbm, 0, buf.at[slot, 0], r, sem.at[slot]).wait()
        _row_copy(y_hbm, 0, buf.at[slot, 1], r, sem.at[slot]).wait()

    _for_each_row(tm, wait_pair)
    for k in range(TOP_K):
        flat[k] = buf[slot, k].reshape(flat.shape[1:])
    g1 = r_ref[:, 2:3]
    g2 = r_ref[:, 3:4]
    y = flat[0] * g1 + flat[1] * g2
    o_ref[...] = x_ref[...] + mod_ref[0, row:row + 1, :] * y


def _moe_combine(x, y_slots, dest, route, mods, row):
    tm = TM_COMB
    grid_spec = pltpu.PrefetchScalarGridSpec(
        num_scalar_prefetch=1,
        grid=(M_TOK // tm,),
        in_specs=[
            pl.BlockSpec(memory_space=pl.ANY),
            pl.BlockSpec((tm, D_MODEL), lambda i, d: (i, 0)),
            pl.BlockSpec((tm, LANES), lambda i, d: (i, 0)),
            pl.BlockSpec((1, MOD_ROWS, D_MODEL), lambda i, d: (_group_of_tile(i, tm), 0, 0)),
        ],
        out_specs=pl.BlockSpec((tm, D_MODEL), lambda i, d: (i, 0)),
        scratch_shapes=[pltpu.VMEM((2, TOP_K, tm, 1, D_MODEL), F32), pltpu.VMEM((TOP_K, tm, D_MODEL), F32),
                        pltpu.SemaphoreType.DMA((2,))],
    )
    return pl.pallas_call(
        functools.partial(_combine_kernel, row=row),
        out_shape=jax.ShapeDtypeStruct((M_TOK, D_MODEL), F32),
        grid_spec=grid_spec,
        compiler_params=_cparams(("arbitrary",), 32 * 2**20),
        name="moe_combine",
    )(dest.reshape(-1), y_slots, x, route, mods)


def _moe_plan(route):
    tm = TM_FFN
    expert = route[:, :TOP_K].astype(jnp.int32).reshape(-1)
    onehot = (expert[:, None] == jnp.arange(N_EXPERTS, dtype=jnp.int32)[None, :]).astype(jnp.int32)
    csum = jnp.cumsum(onehot, axis=0)
    rank = jnp.take_along_axis(csum, expert[:, None], axis=1)[:, 0] - 1
    counts = csum[-1]
    padded = (counts + tm - 1) // tm * tm
    pend = jnp.cumsum(padded)
    pstart = pend - padded
    dest = pstart[expert] + rank
    token = jnp.arange(M_TOK * TOP_K, dtype=jnp.int32) // TOP_K
    slot_token = jnp.zeros((N_SLOTS,), jnp.int32).at[dest].set(token)
    n_valid = pend[-1:] // tm
    run_start = jnp.concatenate([pstart // tm, n_valid]).astype(jnp.int32)
    run_blocks = jnp.concatenate([padded // tm, N_SLOTS_BLOCKS - n_valid]).astype(jnp.int32)
    return slot_token, dest.astype(jnp.int32).reshape(M_TOK, TOP_K), run_start, run_blocks


def kernel(x_prompt, x_sample, cache_k, cache_v, c, c_ctx, ada_w, ada_b, norm1_g, norm2_g, w_in, qn_g, kn_g, pool_w, pool_scale, rel_bias, w_bp, w_ba, w_out, ffn_w1, ffn_w3, ffn_w2, router_w, router_b, moe_w1, moe_w3, moe_w2):
    x = jnp.concatenate([x_prompt.reshape(M_CTX, D_MODEL), x_sample.reshape(M_LAT, D_MODEL)], axis=0)
    cond8 = jnp.zeros((MOD_ROWS, D_MODEL), F32).at[0].set(c_ctx).at[1:1 + DEC_BATCH].set(c)
    mods_all = _adaln_all(cond8, ada_w, ada_b)
    mods_all = mods_all[:, :N_GROUPS].reshape(DEPTH, N_GROUPS, 6, D_MODEL)
    mods_all = jnp.pad(mods_all, ((0, 0), (0, 0), (0, MOD_ROWS - 6), (0, 0)))
    t2_all = _bias_tables(rel_bias)

    new_k = jnp.zeros((BATCH, DEPTH, N_HEADS, SEQ, HEAD_DIM), F32)
    new_v = jnp.zeros((BATCH, DEPTH, N_HEADS, SEQ, HEAD_DIM), F32)
    for l in range(DEPTH):
        j = l // 2
        mods = mods_all[l]
        u = _norm_mod(x, norm1_g[l], mods, 0)
        proj = _in_proj(u, w_in, l)
        pool_out = _pool_mixer(proj, pool_w[l], pool_scale[l])
        attn_ctx, new_k, new_v = _ctx_attention(proj, qn_g[l], kn_g[l], l, new_k, new_v)
        attn_lat = _lat_attention(proj, cache_k, cache_v, t2_all, qn_g[l], kn_g[l], l)
        merged = _mix_merge(pool_out, attn_ctx, attn_lat, w_bp, w_ba, proj, l)
        x = _out_proj_residual(merged, w_out, l, x, mods, 2)
        if l % 2 == 0:
            u2 = _norm_mod(x, norm2_g[l], mods, 3)
            h = _ffn_up(u2, ffn_w1, ffn_w3, j)
            x = _ffn_down(h, ffn_w2, j, x, mods, 5)
        else:
            u2, route = _norm_route(x, norm2_g[l], mods, 3, router_w[j], router_b[j])
            slot_token, dest, run_start, run_blocks = _moe_plan(route)
            xs = _gather_rows(u2, slot_token)
            h = _moe_up(xs, moe_w1, moe_w3, j, run_start, run_blocks)
            y_slots = _moe_down(h, moe_w2, j, run_start, run_blocks)
            x = _moe_combine(x, y_slots, dest, route, mods, 5)

    y_prompt = x[:M_CTX].reshape(BATCH, SEQ, D_MODEL)
    y_sample = x[M_CTX:].reshape(DEC_BATCH, DEC_SEQ, D_MODEL)
    return (y_prompt, y_sample, new_k, new_v)
```

```python
import functools

import jax
import jax.numpy as jnp
from jax import lax
from jax.experimental import pallas as pl
from jax.experimental.pallas import tpu as pltpu

F32 = jnp.float32
BF16 = jnp.bfloat16

D_MODEL = 2048
BATCH = 32
SEQ = 256
DEPTH = 4
DEC_BATCH = 2
DEC_SEQ = 2048
PAST_LEN = 512
GRID_W = 64
GRID_ROWS = DEC_SEQ // GRID_W
POOL_WIDTH = D_MODEL // 2
N_POOL_GROUPS = 4
POOL_GROUP_W = POOL_WIDTH // N_POOL_GROUPS
POOL_WINDOWS = (2, 4, 8, 16)
ATTN_WIDTH = D_MODEL // 2
HEAD_DIM = 64
N_HEADS = ATTN_WIDTH // HEAD_DIM
WIN_ROWS = 8
WIN_COLS = 16
D_FF = ((8 * D_MODEL // 3 + 255) // 256) * 256
N_EXPERTS = 8
TOP_K = 2
PROJ_WIDTH = POOL_WIDTH + 3 * ATTN_WIDTH + 2 * D_MODEL
RMS_EPS = 1e-6
NEG_INF = -1e30
ATTN_SCALE = HEAD_DIM ** -0.5

M_CTX = BATCH * SEQ
M_LAT = DEC_BATCH * DEC_SEQ
M_TOK = M_CTX + M_LAT
N_GROUPS = 1 + DEC_BATCH
MOD_ROWS = 8

LANES = 128
V7X_VMEM_BYTES = 64 * 1024 * 1024
VMEM_BUDGET = V7X_VMEM_BYTES - 8 * 1024 * 1024

CAST_ROWS = 256

TM_NORM = 512
TM_IN, TN_IN = 1024, 1024
TM_MIX, TN_MIX = 512, 1024
TM_OUT, TN_OUT = 1024, 1024
TM_FFN, TN_FFN1, TN_FFN2 = 512, 512, 512
TM_FFN_UP_DENSE = 1024
TM_COMB = 256
POOL_TILE = 256
POOL_HALO = 8
LAT_QROWS = 8
LAT_QBLK = LAT_QROWS * GRID_W
LAT_KROWS = 16
LAT_KWIN = LAT_KROWS * GRID_W
N_SLOTS_BLOCKS = (M_TOK * TOP_K) // TM_FFN + N_EXPERTS
N_SLOTS = N_SLOTS_BLOCKS * TM_FFN


def _cparams(semantics, vmem_bytes):
    return pltpu.CompilerParams(dimension_semantics=semantics, vmem_limit_bytes=int(vmem_bytes))


def _group_of_tile(m, tm):
    start = m * tm
    return jnp.where(start < M_CTX, 0, 1 + (start - M_CTX) // DEC_SEQ)


def _sigmoid(z):
    return 1.0 / (1.0 + jnp.exp(-z))


def _silu(z):
    return z / (1.0 + jnp.exp(-z))


def _rms(x):
    return x * lax.rsqrt(jnp.mean(x * x, axis=-1, keepdims=True) + RMS_EPS)


def _cast_weight(w_ref, scr_ref, k_rows):
    def body(i, carry):
        r = pl.multiple_of(i * CAST_ROWS, CAST_ROWS)
        scr_ref[pl.ds(r, CAST_ROWS), :] = w_ref[pl.ds(r, CAST_ROWS), :].astype(BF16)
        return carry

    lax.fori_loop(0, k_rows // CAST_ROWS, body, 0)


def _ada_kernel(c_ref, w_ref, b_ref, o_ref):
    s = _silu(c_ref[...]).astype(BF16)
    w = w_ref[0].astype(BF16)
    o_ref[0] = jnp.dot(s, w, preferred_element_type=F32) + b_ref[0]


def _adaln_all(cond8, ada_w, ada_b):
    tn = 1024
    n_out = 6 * D_MODEL
    return pl.pallas_call(
        _ada_kernel,
        out_shape=jax.ShapeDtypeStruct((DEPTH, MOD_ROWS, n_out), F32),
        grid=(DEPTH, n_out // tn),
        in_specs=[
            pl.BlockSpec((MOD_ROWS, D_MODEL), lambda l, n: (0, 0)),
            pl.BlockSpec((1, D_MODEL, tn), lambda l, n: (l, 0, n)),
            pl.BlockSpec((1, 1, tn), lambda l, n: (l, 0, n)),
        ],
        out_specs=pl.BlockSpec((1, MOD_ROWS, tn), lambda l, n: (l, 0, n)),
        compiler_params=_cparams(("arbitrary", "arbitrary"), 40 * 2**20),
        name="adaln",
    )(cond8, ada_w, ada_b.reshape(DEPTH, 1, n_out))


def _norm_mod_kernel(x_ref, g_ref, mod_ref, o_ref, *, row):
    y = _rms(x_ref[...]) * g_ref[...]
    sh = mod_ref[0, row:row + 1, :]
    sc = mod_ref[0, row + 1:row + 2, :]
    o_ref[...] = (y * (1.0 + sc) + sh).astype(o_ref.dtype)


def _norm_mod(x, g, mods, row):
    tm = TM_NORM
    return pl.pallas_call(
        functools.partial(_norm_mod_kernel, row=row),
        out_shape=jax.ShapeDtypeStruct((M_TOK, D_MODEL), BF16),
        grid=(M_TOK // tm,),
        in_specs=[
            pl.BlockSpec((tm, D_MODEL), lambda m: (m, 0)),
            pl.BlockSpec((1, D_MODEL), lambda m: (0, 0)),
            pl.BlockSpec((1, MOD_ROWS, D_MODEL), lambda m: (_group_of_tile(m, tm), 0, 0)),
        ],
        out_specs=pl.BlockSpec((tm, D_MODEL), lambda m: (m, 0)),
        compiler_params=_cparams(("parallel",), 32 * 2**20),
        name="norm_mod",
    )(x, g.reshape(1, D_MODEL), mods)


def _norm_route_kernel(x_ref, g_ref, mod_ref, rwh_ref, rwl_ref, rb_ref, u_ref, r_ref, *, row):
    y = _rms(x_ref[...]) * g_ref[...]
    sh = mod_ref[0, row:row + 1, :]
    sc = mod_ref[0, row + 1:row + 2, :]
    u = y * (1.0 + sc) + sh
    u_ref[...] = u.reshape(u_ref.shape)
    u_hi = u.astype(BF16)
    u_lo = (u - u_hi.astype(F32)).astype(BF16)
    logits = (jnp.dot(u_hi, rwh_ref[...], preferred_element_type=F32)
              + jnp.dot(u_hi, rwl_ref[...], preferred_element_type=F32)
              + jnp.dot(u_lo, rwh_ref[...], preferred_element_type=F32)) + rb_ref[...]
    lane = lax.broadcasted_iota(jnp.int32, logits.shape, 1).astype(F32)
    big = float(LANES)
    m1 = jnp.max(logits, axis=-1, keepdims=True)
    i1 = jnp.min(jnp.where(logits == m1, lane, big), axis=-1, keepdims=True)
    rest = jnp.where(lane == i1, -jnp.inf, logits)
    m2 = jnp.max(rest, axis=-1, keepdims=True)
    i2 = jnp.min(jnp.where(rest == m2, lane, big), axis=-1, keepdims=True)
    e2 = jnp.exp(m2 - m1)
    den = 1.0 + e2
    r_ref[...] = jnp.where(lane == 0.0, i1,
                           jnp.where(lane == 1.0, i2,
                                     jnp.where(lane == 2.0, 1.0 / den,
                                               jnp.where(lane == 3.0, e2 / den, 0.0))))


def _norm_route(x, g, mods, row, router_w, router_b):
    tm = TM_NORM
    rw = jnp.zeros((D_MODEL, LANES), F32).at[:, :N_EXPERTS].set(router_w)
    rw_hi = rw.astype(BF16)
    rw_lo = (rw - rw_hi.astype(F32)).astype(BF16)
    rb = jnp.full((1, LANES), -jnp.inf, F32).at[0, :N_EXPERTS].set(router_b)
    return pl.pallas_call(
        functools.partial(_norm_route_kernel, row=row),
        out_shape=(jax.ShapeDtypeStruct((M_TOK, 1, D_MODEL), F32),
                   jax.ShapeDtypeStruct((M_TOK, LANES), F32)),
        grid=(M_TOK // tm,),
        in_specs=[
            pl.BlockSpec((tm, D_MODEL), lambda m: (m, 0)),
            pl.BlockSpec((1, D_MODEL), lambda m: (0, 0)),
            pl.BlockSpec((1, MOD_ROWS, D_MODEL), lambda m: (_group_of_tile(m, tm), 0, 0)),
            pl.BlockSpec((D_MODEL, LANES), lambda m: (0, 0)),
            pl.BlockSpec((D_MODEL, LANES), lambda m: (0, 0)),
            pl.BlockSpec((1, LANES), lambda m: (0, 0)),
        ],
        out_specs=(pl.BlockSpec((tm, 1, D_MODEL), lambda m: (m, 0, 0)),
                   pl.BlockSpec((tm, LANES), lambda m: (m, 0))),
        compiler_params=_cparams(("parallel",), 40 * 2**20),
        name="norm_route",
    )(x, g.reshape(1, D_MODEL), mods, rw_hi, rw_lo, rb)


def _in_proj_kernel(a_ref, w_ref, o_ref, wscr):
    @pl.when(pl.program_id(1) == 0)
    def _():
        _cast_weight(w_ref.at[0], wscr, D_MODEL)

    o_ref[...] = jnp.dot(a_ref[...], wscr[...], preferred_element_type=F32)


def _in_proj(u, w_in, layer):
    tm, tn = TM_IN, TN_IN
    return pl.pallas_call(
        _in_proj_kernel,
        out_shape=jax.ShapeDtypeStruct((M_TOK, PROJ_WIDTH), F32),
        grid=(PROJ_WIDTH // tn, M_TOK // tm),
        in_specs=[
            pl.BlockSpec((tm, D_MODEL), lambda n, m: (m, 0)),
            pl.BlockSpec((1, D_MODEL, tn), lambda n, m: (layer, 0, n)),
        ],
        out_specs=pl.BlockSpec((tm, tn), lambda n, m: (m, n)),
        scratch_shapes=[pltpu.VMEM((D_MODEL, tn), BF16)],
        compiler_params=_cparams(("arbitrary", "arbitrary"), VMEM_BUDGET),
        name="in_proj",
    )(u, w_in)


def _pool_kernel(xm_ref, xp_ref, xn_ref, pw_ref, ps_ref, o_ref, pad):
    i = pl.program_id(0)
    n_ctx_tiles = M_CTX // POOL_TILE
    tiles_per_lat = DEC_SEQ // POOL_TILE
    is_lat = i >= n_ctx_tiles
    j = (i - n_ctx_tiles) % tiles_per_lat
    has_prev = jnp.logical_and(is_lat, j != 0)
    has_next = jnp.logical_and(is_lat, j != tiles_per_lat - 1)
    pad[0:POOL_HALO, :] = jnp.where(has_prev, xp_ref[...], 0.0)
    pad[POOL_HALO:POOL_HALO + POOL_TILE, :] = xm_ref[...]
    pad[POOL_HALO + POOL_TILE:, :] = jnp.where(has_next, xn_ref[...], 0.0)
    off = jnp.where(is_lat, j * POOL_TILE, 0)
    seq_len = jnp.where(is_lat, DEC_SEQ, SEQ)
    pos = lax.broadcasted_iota(jnp.int32, (POOL_TILE, 1), 0) + off
    for g, win in enumerate(POOL_WINDOWS):
        lanes = slice(g * POOL_GROUP_W, (g + 1) * POOL_GROUP_W)
        back, fwd = win // 2, win - win // 2
        acc = pad[pl.ds(POOL_HALO - back, POOL_TILE), lanes]
        for d in range(-back + 1, fwd):
            acc = acc + pad[pl.ds(POOL_HALO + d, POOL_TILE), lanes]
        cnt = (jnp.minimum(pos + fwd, seq_len) - jnp.maximum(pos - back, 0)).astype(F32)
        pooled = (acc / cnt - xm_ref[:, lanes]).astype(BF16)
        mixed = jnp.dot(pooled, pw_ref[g].astype(BF16), preferred_element_type=F32)
        o_ref[:, lanes] = (mixed * ps_ref[:, lanes]).astype(BF16)


def _pool_mixer(proj, pool_w, pool_scale):
    t, h = POOL_TILE, POOL_HALO
    hb = t // h
    last_halo = M_TOK // h - 1
    return pl.pallas_call(
        _pool_kernel,
        out_shape=jax.ShapeDtypeStruct((M_TOK, POOL_WIDTH), BF16),
        grid=(M_TOK // t,),
        in_specs=[
            pl.BlockSpec((t, POOL_WIDTH), lambda i: (i, 0)),
            pl.BlockSpec((h, POOL_WIDTH), lambda i: (jnp.maximum(i * hb - 1, 0), 0)),
            pl.BlockSpec((h, POOL_WIDTH), lambda i: (jnp.minimum((i + 1) * hb, last_halo), 0)),
            pl.BlockSpec((N_POOL_GROUPS, POOL_GROUP_W, POOL_GROUP_W), lambda i: (0, 0, 0)),
            pl.BlockSpec((1, POOL_WIDTH), lambda i: (0, 0)),
        ],
        out_specs=pl.BlockSpec((t, POOL_WIDTH), lambda i: (i, 0)),
        scratch_shapes=[pltpu.VMEM((t + 2 * h, POOL_WIDTH), F32)],
        compiler_params=_cparams(("parallel",), 24 * 2**20),
        name="pool_mixer",
    )(proj, proj, proj, pool_w, pool_scale.reshape(1, POOL_WIDTH))


PAIR_W = 2 * HEAD_DIM


def _pair_group_ones():
    r = lax.broadcasted_iota(jnp.int32, (PAIR_W, PAIR_W), 0) // HEAD_DIM
    c = lax.broadcasted_iota(jnp.int32, (PAIR_W, PAIR_W), 1) // HEAD_DIM
    return jnp.where(r == c, 1.0, 0.0).astype(BF16)


def _head_masks():
    lane = lax.broadcasted_iota(jnp.int32, (1, PAIR_W), 1)
    return (lane < HEAD_DIM, lane >= HEAD_DIM)


def _rms_pair(x, ones):
    y2 = x * x
    hi = y2.astype(BF16)
    lo = (y2 - hi.astype(F32)).astype(BF16)
    ss = jnp.dot(hi, ones, preferred_element_type=F32) + jnp.dot(lo, ones, preferred_element_type=F32)
    return x * lax.rsqrt(ss * (1.0 / HEAD_DIM) + RMS_EPS)


def _softmax_rows(s):
    e = jnp.exp(s - jnp.max(s, axis=-1, keepdims=True))
    return e * (1.0 / jnp.sum(e, axis=-1, keepdims=True))


_NT_DIMS = (((1,), (1,)), ((), ()))


def _ctx_attn_kernel(q_ref, k_ref, v_ref, qg_ref, kg_ref, _ck_in, _cv_in, o_ref, ko_ref, vo_ref):
    qg = qg_ref[...] * ATTN_SCALE
    kg = kg_ref[...]
    ones = _pair_group_ones()
    masks = _head_masks()
    for p in range(N_HEADS // 2):
        sl = slice(p * PAIR_W, (p + 1) * PAIR_W)
        kn = _rms_pair(k_ref[:, sl], ones) * kg
        v = v_ref[:, sl]
        for hh in range(2):
            lanes = slice(hh * HEAD_DIM, (hh + 1) * HEAD_DIM)
            ko_ref[0, 0, 2 * p + hh] = kn[:, lanes]
            vo_ref[0, 0, 2 * p + hh] = v[:, lanes]
        qn = _rms_pair(q_ref[:, sl], ones) * qg
        kb = kn.astype(BF16)
        o = jnp.zeros((SEQ, PAIR_W), F32)
        for hh in range(2):
            qh = jnp.where(masks[hh], qn, 0.0).astype(BF16)
            s = lax.dot_general(qh, kb, _NT_DIMS, preferred_element_type=F32)
            pr = _softmax_rows(s).astype(BF16)
            vh = jnp.where(masks[hh], v, 0.0).astype(BF16)
            o = o + jnp.dot(pr, vh, preferred_element_type=F32)
        o_ref[:, sl] = o.astype(BF16)


def _pair_gain(g):
    return jnp.concatenate([g, g]).reshape(1, PAIR_W)


def _ctx_attention(proj, qn_g, kn_g, layer, new_k, new_v):
    q_blk = POOL_WIDTH // ATTN_WIDTH
    cache_shape = jax.ShapeDtypeStruct((BATCH, DEPTH, N_HEADS, SEQ, HEAD_DIM), F32)
    cache_spec = pl.BlockSpec((1, 1, N_HEADS, SEQ, HEAD_DIM), lambda b: (b, layer, 0, 0, 0))
    return pl.pallas_call(
        _ctx_attn_kernel,
        out_shape=(jax.ShapeDtypeStruct((M_CTX, ATTN_WIDTH), BF16), cache_shape, cache_shape),
        grid=(BATCH,),
        in_specs=[
            pl.BlockSpec((SEQ, ATTN_WIDTH), lambda b: (b, q_blk)),
            pl.BlockSpec((SEQ, ATTN_WIDTH), lambda b: (b, q_blk + 1)),
            pl.BlockSpec((SEQ, ATTN_WIDTH), lambda b: (b, q_blk + 2)),
            pl.BlockSpec((1, PAIR_W), lambda b: (0, 0)),
            pl.BlockSpec((1, PAIR_W), lambda b: (0, 0)),
            pl.BlockSpec(memory_space=pl.ANY),
            pl.BlockSpec(memory_space=pl.ANY),
        ],
        out_specs=(pl.BlockSpec((SEQ, ATTN_WIDTH), lambda b: (b, 0)), cache_spec, cache_spec),
        input_output_aliases={5: 1, 6: 2},
        compiler_params=_cparams(("arbitrary",), 40 * 2**20),
        name="ctx_attention",
    )(proj, proj, proj, _pair_gain(qn_g), _pair_gain(kn_g), new_k, new_v)


def _lat_attn_kernel(q_ref, k_ref, v_ref, ck_ref, cv_ref, t2_ref, qg_ref, kg_ref, o_ref,
                     kscr, vscr, ckscr, cvscr, sscr):
    blk = pl.program_id(2)

    ones = _pair_group_ones()
    masks = _head_masks()

    @pl.when(blk == 0)
    def _():
        kscr[...] = (_rms_pair(k_ref[...], ones) * kg_ref[...]).astype(BF16)
        ckscr[...] = jnp.concatenate([ck_ref[0, 0, 0], ck_ref[0, 0, 1]], axis=-1).astype(BF16)
        v = v_ref[...]
        cv = jnp.concatenate([cv_ref[0, 0, 0], cv_ref[0, 0, 1]], axis=-1)
        for hh in range(2):
            vscr[hh] = jnp.where(masks[hh], v, 0.0).astype(BF16)
            cvscr[hh] = jnp.where(masks[hh], cv, 0.0).astype(BF16)

    r0 = blk * LAT_QROWS
    w0 = jnp.clip(r0 - WIN_ROWS // 2, 0, GRID_ROWS - LAT_KROWS)
    kstart = pl.multiple_of(w0 * GRID_W, 256)
    lane_lo = lax.broadcasted_iota(jnp.int32, (GRID_W, 2 * GRID_W), 1) < GRID_W
    qn = _rms_pair(q_ref[...], ones) * (qg_ref[...] * ATTN_SCALE)
    kw = kscr[pl.ds(kstart, LAT_KWIN), :]
    ckb = ckscr[...]
    o = jnp.zeros((LAT_QBLK, PAIR_W), F32)
    for hh in range(2):
        qh = jnp.where(masks[hh], qn, 0.0).astype(BF16)
        s = lax.dot_general(qh, kw, _NT_DIMS, preferred_element_type=F32)
        for qi in range(LAT_QROWS):
            qr = r0 + qi
            rs = jnp.clip(qr - WIN_ROWS // 2, 0, GRID_ROWS - WIN_ROWS)
            for jp in range(LAT_KROWS // 2):
                kr0 = w0 + 2 * jp
                idx = jnp.clip(kr0 - qr + WIN_ROWS, 0, 2 * WIN_ROWS - 1)
                ok0 = jnp.logical_and(kr0 >= rs, kr0 < rs + WIN_ROWS).astype(jnp.int32)
                ok1 = jnp.logical_and(kr0 + 1 >= rs, kr0 + 1 < rs + WIN_ROWS).astype(jnp.int32)
                ok = jnp.where(lane_lo, ok0, ok1) > 0
                tile = jnp.where(ok, t2_ref[0, hh, idx], NEG_INF)
                rows = slice(qi * GRID_W, (qi + 1) * GRID_W)
                cols = slice(jp * 2 * GRID_W, (jp + 1) * 2 * GRID_W)
                sscr[rows, cols] = s[rows, cols] + tile
        s_loc = sscr[...]
        s_ctx = lax.dot_general(qh, ckb, _NT_DIMS, preferred_element_type=F32)
        m = jnp.maximum(jnp.max(s_loc, axis=-1, keepdims=True), jnp.max(s_ctx, axis=-1, keepdims=True))
        e_loc = jnp.exp(s_loc - m)
        e_ctx = jnp.exp(s_ctx - m)
        inv = 1.0 / (jnp.sum(e_loc, axis=-1, keepdims=True) + jnp.sum(e_ctx, axis=-1, keepdims=True))
        vh = vscr[hh, pl.ds(kstart, LAT_KWIN), :]
        o = o + jnp.dot((e_loc * inv).astype(BF16), vh, preferred_element_type=F32)
        o = o + jnp.dot((e_ctx * inv).astype(BF16), cvscr[hh], preferred_element_type=F32)
    o_ref[...] = o.astype(BF16)


def _bias_tables(rel_bias):
    col = jnp.arange(GRID_W)
    col_start = jnp.clip(col - WIN_COLS // 2, 0, GRID_W - WIN_COLS)
    col_mask = (col[None, :] >= col_start[:, None]) & (col[None, :] < col_start[:, None] + WIN_COLS)
    dc_idx = jnp.clip(col[None, :] - col[:, None], -(WIN_COLS - 1), WIN_COLS - 1) + (WIN_COLS - 1)
    onehot = (dc_idx[:, :, None] == jnp.arange(2 * WIN_COLS - 1)[None, None, :]).astype(F32)
    t = jnp.einsum('lhrc,qkc->lhrqk', rel_bias.astype(F32), onehot, precision=lax.Precision.HIGHEST)
    t = jnp.where(col_mask, t, NEG_INF)
    neg = jnp.full(t.shape[:2] + (1,) + t.shape[3:], NEG_INF, F32)
    tp = jnp.concatenate([neg, t, neg], axis=2)
    return jnp.concatenate([tp[:, :, :2 * WIN_ROWS], tp[:, :, 1:2 * WIN_ROWS + 1]], axis=-1)


def _lat_attention(proj, cache_k, cache_v, t2, qn_g, kn_g, layer):
    hp_w = 2 * HEAD_DIM
    q_col = POOL_WIDTH // hp_w
    k_col = (POOL_WIDTH + ATTN_WIDTH) // hp_w
    v_col = (POOL_WIDTH + 2 * ATTN_WIDTH) // hp_w
    lat_row_seq = M_CTX // DEC_SEQ
    lat_row_blk = M_CTX // LAT_QBLK
    n_blk = DEC_SEQ // LAT_QBLK
    return pl.pallas_call(
        _lat_attn_kernel,
        out_shape=jax.ShapeDtypeStruct((M_LAT, ATTN_WIDTH), BF16),
        grid=(DEC_BATCH, N_HEADS // 2, n_blk),
        in_specs=[
            pl.BlockSpec((LAT_QBLK, hp_w), lambda b, hp, blk: (lat_row_blk + b * n_blk + blk, q_col + hp)),
            pl.BlockSpec((DEC_SEQ, hp_w), lambda b, hp, blk: (lat_row_seq + b, k_col + hp)),
            pl.BlockSpec((DEC_SEQ, hp_w), lambda b, hp, blk: (lat_row_seq + b, v_col + hp)),
            pl.BlockSpec((1, 1, 2, PAST_LEN, HEAD_DIM), lambda b, hp, blk: (b, layer, hp, 0, 0)),
            pl.BlockSpec((1, 1, 2, PAST_LEN, HEAD_DIM), lambda b, hp, blk: (b, layer, hp, 0, 0)),
            pl.BlockSpec((1, 2, 2 * WIN_ROWS, GRID_W, 2 * GRID_W), lambda b, hp, blk: (layer, hp, 0, 0, 0)),
            pl.BlockSpec((1, PAIR_W), lambda b, hp, blk: (0, 0)),
            pl.BlockSpec((1, PAIR_W), lambda b, hp, blk: (0, 0)),
        ],
        out_specs=pl.BlockSpec((LAT_QBLK, hp_w), lambda b, hp, blk: (b * n_blk + blk, hp)),
        scratch_shapes=[
            pltpu.VMEM((DEC_SEQ, PAIR_W), BF16),
            pltpu.VMEM((2, DEC_SEQ, PAIR_W), BF16),
            pltpu.VMEM((PAST_LEN, PAIR_W), BF16),
            pltpu.VMEM((2, PAST_LEN, PAIR_W), BF16),
            pltpu.VMEM((LAT_QBLK, LAT_KWIN), F32),
        ],
        compiler_params=_cparams(("arbitrary", "arbitrary", "arbitrary"), 48 * 2**20),
        name="lat_attention",
    )(proj, proj, proj, cache_k, cache_v, t2, _pair_gain(qn_g), _pair_gain(kn_g))


def _mix_kernel(p_ref, ac_ref, al_ref, wbp_ref, wba_ref, gp_ref, ga_ref, o_ref, sbp, sba):
    m = pl.program_id(1)

    @pl.when(m == 0)
    def _():
        _cast_weight(wbp_ref.at[0], sbp, POOL_WIDTH)
        _cast_weight(wba_ref.at[0], sba, ATTN_WIDTH)

    a = jnp.where(m < M_CTX // TM_MIX, ac_ref[...], al_ref[...])
    yp = jnp.dot(p_ref[...], sbp[...], preferred_element_type=F32)
    ya = jnp.dot(a, sba[...], preferred_element_type=F32)
    o_ref[...] = (_sigmoid(gp_ref[...]) * yp + _sigmoid(ga_ref[...]) * ya).astype(BF16)


def _mix_merge(pool_out, attn_ctx, attn_lat, w_bp, w_ba, proj, layer):
    tm, tn = TM_MIX, TN_MIX
    gp_col = (POOL_WIDTH + 3 * ATTN_WIDTH) // tn
    ga_col = gp_col + D_MODEL // tn
    n_ctx = M_CTX // tm
    return pl.pallas_call(
        _mix_kernel,
        out_shape=jax.ShapeDtypeStruct((M_TOK, D_MODEL), BF16),
        grid=(D_MODEL // tn, M_TOK // tm),
        in_specs=[
            pl.BlockSpec((tm, POOL_WIDTH), lambda n, m: (m, 0)),
            pl.BlockSpec((tm, ATTN_WIDTH), lambda n, m: (jnp.minimum(m, n_ctx - 1), 0)),
            pl.BlockSpec((tm, ATTN_WIDTH), lambda n, m: (jnp.maximum(m - n_ctx, 0), 0)),
            pl.BlockSpec((1, POOL_WIDTH, tn), lambda n, m: (layer, 0, n)),
            pl.BlockSpec((1, ATTN_WIDTH, tn), lambda n, m: (layer, 0, n)),
            pl.BlockSpec((tm, tn), lambda n, m: (m, gp_col + n)),
            pl.BlockSpec((tm, tn), lambda n, m: (m, ga_col + n)),
        ],
        out_specs=pl.BlockSpec((tm, tn), lambda n, m: (m, n)),
        scratch_shapes=[pltpu.VMEM((POOL_WIDTH, tn), BF16), pltpu.VMEM((ATTN_WIDTH, tn), BF16)],
        compiler_params=_cparams(("arbitrary", "arbitrary"), 48 * 2**20),
        name="mix_merge",
    )(pool_out, attn_ctx, attn_lat, w_bp, w_ba, proj, proj)


def _out_proj_kernel(a_ref, w_ref, x_ref, mod_ref, o_ref, wscr, *, row):
    @pl.when(pl.program_id(1) == 0)
    def _():
        _cast_weight(w_ref.at[0], wscr, D_MODEL)

    y = jnp.dot(a_ref[...], wscr[...], preferred_element_type=F32)
    o_ref[...] = x_ref[...] + mod_ref[0, row:row + 1, :] * y


def _out_proj_residual(merged, w_out, layer, x, mods, row):
    tm, tn = TM_OUT, TN_OUT
    return pl.pallas_call(
        functools.partial(_out_proj_kernel, row=row),
        out_shape=jax.ShapeDtypeStruct((M_TOK, D_MODEL), F32),
        grid=(D_MODEL // tn, M_TOK // tm),
        in_specs=[
            pl.BlockSpec((tm, D_MODEL), lambda n, m: (m, 0)),
            pl.BlockSpec((1, D_MODEL, tn), lambda n, m: (layer, 0, n)),
            pl.BlockSpec((tm, tn), lambda n, m: (m, n)),
            pl.BlockSpec((1, MOD_ROWS, tn), lambda n, m: (_group_of_tile(m, tm), 0, n)),
        ],
        out_specs=pl.BlockSpec((tm, tn), lambda n, m: (m, n)),
        scratch_shapes=[pltpu.VMEM((D_MODEL, tn), BF16)],
        compiler_params=_cparams(("arbitrary", "arbitrary"), VMEM_BUDGET),
        name="out_proj",
    )(merged, w_out, x, mods)


def _ffn_up_kernel(a_ref, w1_ref, w3_ref, o_ref, s1, s3):
    @pl.when(pl.program_id(1) == 0)
    def _():
        _cast_weight(w1_ref.at[0], s1, D_MODEL)
        _cast_weight(w3_ref.at[0], s3, D_MODEL)

    a = a_ref[...]
    h1 = jnp.dot(a, s1[...], preferred_element_type=F32)
    h3 = jnp.dot(a, s3[...], preferred_element_type=F32)
    o_ref[...] = (_silu(h1) * h3).astype(BF16)


def _ffn_up(a, w1, w3, j):
    tm, tn = TM_FFN_UP_DENSE, TN_FFN1
    return pl.pallas_call(
        _ffn_up_kernel,
        out_shape=jax.ShapeDtypeStruct((M_TOK, D_FF), BF16),
        grid=(D_FF // tn, M_TOK // tm),
        in_specs=[
            pl.BlockSpec((tm, D_MODEL), lambda n, m: (m, 0)),
            pl.BlockSpec((1, D_MODEL, tn), lambda n, m: (j, 0, n)),
            pl.BlockSpec((1, D_MODEL, tn), lambda n, m: (j, 0, n)),
        ],
        out_specs=pl.BlockSpec((tm, tn), lambda n, m: (m, n)),
        scratch_shapes=[pltpu.VMEM((D_MODEL, tn), BF16), pltpu.VMEM((D_MODEL, tn), BF16)],
        compiler_params=_cparams(("arbitrary", "arbitrary"), 48 * 2**20),
        name="ffn_up",
    )(a, w1, w3)


def _ffn_down_kernel(a_ref, w_ref, x_ref, mod_ref, o_ref, wscr, *, row):
    @pl.when(pl.program_id(1) == 0)
    def _():
        _cast_weight(w_ref.at[0], wscr, D_FF)

    y = jnp.dot(a_ref[...], wscr[...], preferred_element_type=F32)
    o_ref[...] = x_ref[...] + mod_ref[0, row:row + 1, :] * y


def _ffn_down(h, w2, j, x, mods, row):
    tm, tn = TM_FFN, TN_FFN2
    return pl.pallas_call(
        functools.partial(_ffn_down_kernel, row=row),
        out_shape=jax.ShapeDtypeStruct((M_TOK, D_MODEL), F32),
        grid=(D_MODEL // tn, M_TOK // tm),
        in_specs=[
            pl.BlockSpec((tm, D_FF), lambda n, m: (m, 0)),
            pl.BlockSpec((1, D_FF, tn), lambda n, m: (j, 0, n)),
            pl.BlockSpec((tm, tn), lambda n, m: (m, n)),
            pl.BlockSpec((1, MOD_ROWS, tn), lambda n, m: (_group_of_tile(m, tm), 0, n)),
        ],
        out_specs=pl.BlockSpec((tm, tn), lambda n, m: (m, n)),
        scratch_shapes=[pltpu.VMEM((D_FF, tn), BF16)],
        compiler_params=_cparams(("arbitrary", "arbitrary"), VMEM_BUDGET),
        name="ffn_down",
    )(h, w2, x, mods)


N_RUNS = N_EXPERTS + 1


def _run_blocks_pipelined(nb, a_copy, o_copy, compute):
    def body(k, carry):
        slot = k % 2

        @pl.when(k + 1 < nb)
        def _():
            a_copy(k + 1, 1 - slot).start()

        a_copy(k, slot).wait()

        @pl.when(k >= 2)
        def _():
            o_copy(k - 2, slot).wait()

        compute(slot)
        o_copy(k, slot).start()
        return carry

    lax.fori_loop(0, nb, body, 0)

    @pl.when(nb >= 2)
    def _():
        o_copy(nb - 2, nb % 2).wait()

    @pl.when(nb >= 1)
    def _():
        o_copy(nb - 1, (nb - 1) % 2).wait()


def _zero_tail_blocks(nb, obuf, o_copy):
    obuf[0] = jnp.zeros(obuf.shape[1:], obuf.dtype)

    def body(k, carry):
        o_copy(k, 0).start()
        o_copy(k, 0).wait()
        return carry

    lax.fori_loop(0, nb, body, 0)


def _weights_for_step(n, r, n_tiles, w_copies):
    step = n * N_EXPERTS + r
    slot = step % 2

    @pl.when(step == 0)
    def _():
        for c in w_copies(0, 0, 0):
            c.start()

    nxt = step + 1

    @pl.when(nxt < n_tiles * N_EXPERTS)
    def _():
        for c in w_copies(nxt // N_EXPERTS, nxt % N_EXPERTS, 1 - slot):
            c.start(priority=1)

    for c in w_copies(n, r, slot):
        c.wait()
    return slot


def _moe_up_kernel(bs_ref, nb_ref, a_hbm, w1_hbm, w3_hbm, h_hbm, wbuf, s1, s3, abuf, obuf, sem_w, sem_a, sem_o, *, j):
    n = pl.program_id(0)
    r = pl.program_id(1)
    b0 = bs_ref[r]
    nb = nb_ref[r]
    tm, tn = obuf.shape[1], obuf.shape[2]
    col = pl.multiple_of(n * tn, tn)

    def w_copies(nn, rr, slot):
        c = pl.multiple_of(nn * tn, tn)
        return (pltpu.make_async_copy(w1_hbm.at[j, rr, :, pl.ds(c, tn)], wbuf.at[slot, 0], sem_w.at[slot]),
                pltpu.make_async_copy(w3_hbm.at[j, rr, :, pl.ds(c, tn)], wbuf.at[slot, 1], sem_w.at[slot]))

    def a_copy(k, slot):
        row = pl.multiple_of((b0 + k) * tm, tm)
        return pltpu.make_async_copy(a_hbm.at[pl.ds(row, tm)], abuf.at[slot], sem_a.at[slot])

    def o_copy(k, slot):
        row = pl.multiple_of((b0 + k) * tm, tm)
        return pltpu.make_async_copy(obuf.at[slot], h_hbm.at[pl.ds(row, tm), pl.ds(col, tn)], sem_o.at[slot])

    def compute(slot):
        a = abuf[slot]
        h1 = jnp.dot(a, s1[...], preferred_element_type=F32)
        h3 = jnp.dot(a, s3[...], preferred_element_type=F32)
        obuf[slot] = (_silu(h1) * h3).astype(BF16)

    @pl.when(r < N_EXPERTS)
    def _():
        @pl.when(nb > 0)
        def _():
            a_copy(0, 0).start()

        slot = _weights_for_step(n, r, pl.num_programs(0), w_copies)

        @pl.when(nb > 0)
        def _():
            _cast_weight(wbuf.at[slot, 0], s1, D_MODEL)
            _cast_weight(wbuf.at[slot, 1], s3, D_MODEL)
            _run_blocks_pipelined(nb, a_copy, o_copy, compute)

    @pl.when(r == N_EXPERTS)
    def _():
        _zero_tail_blocks(nb, obuf, o_copy)


def _moe_up(xs, w1, w3, j, run_start, run_blocks):
    tm, tn = TM_FFN, TN_FFN1
    any_spec = pl.BlockSpec(memory_space=pl.ANY)
    grid_spec = pltpu.PrefetchScalarGridSpec(
        num_scalar_prefetch=2,
        grid=(D_FF // tn, N_RUNS),
        in_specs=[any_spec, any_spec, any_spec],
        out_specs=any_spec,
        scratch_shapes=[
            pltpu.VMEM((2, 2, D_MODEL, tn), F32),
            pltpu.VMEM((D_MODEL, tn), BF16), pltpu.VMEM((D_MODEL, tn), BF16),
            pltpu.VMEM((2, tm, D_MODEL), BF16), pltpu.VMEM((2, tm, tn), BF16),
            pltpu.SemaphoreType.DMA((2,)), pltpu.SemaphoreType.DMA((2,)), pltpu.SemaphoreType.DMA((2,)),
        ],
    )
    return pl.pallas_call(
        functools.partial(_moe_up_kernel, j=j),
        out_shape=jax.ShapeDtypeStruct((N_SLOTS, D_FF), BF16),
        grid_spec=grid_spec,
        compiler_params=_cparams(("arbitrary", "arbitrary"), 40 * 2**20),
        name="moe_up",
    )(run_start, run_blocks, xs, w1, w3)


def _moe_down_kernel(bs_ref, nb_ref, h_hbm, w_hbm, y_hbm, wbuf, wscr, abuf, obuf, sem_w, sem_a, sem_o, *, j):
    n = pl.program_id(0)
    r = pl.program_id(1)
    b0 = bs_ref[r]
    nb = nb_ref[r]
    tm, tn = obuf.shape[1], obuf.shape[3]
    col = pl.multiple_of(n * tn, tn)

    def w_copies(nn, rr, slot):
        c = pl.multiple_of(nn * tn, tn)
        return (pltpu.make_async_copy(w_hbm.at[j, rr, :, pl.ds(c, tn)], wbuf.at[slot], sem_w.at[slot]),)

    def a_copy(k, slot):
        row = pl.multiple_of((b0 + k) * tm, tm)
        return pltpu.make_async_copy(h_hbm.at[pl.ds(row, tm)], abuf.at[slot], sem_a.at[slot])

    def o_copy(k, slot):
        row = pl.multiple_of((b0 + k) * tm, tm)
        return pltpu.make_async_copy(obuf.at[slot], y_hbm.at[pl.ds(row, tm), :, pl.ds(col, tn)], sem_o.at[slot])

    def compute(slot):
        y = jnp.dot(abuf[slot], wscr[...], preferred_element_type=F32)
        obuf[slot] = y.reshape(obuf.shape[1:])

    @pl.when(r < N_EXPERTS)
    def _():
        @pl.when(nb > 0)
        def _():
            a_copy(0, 0).start()

        slot = _weights_for_step(n, r, pl.num_programs(0), w_copies)

        @pl.when(nb > 0)
        def _():
            _cast_weight(wbuf.at[slot], wscr, D_FF)
            _run_blocks_pipelined(nb, a_copy, o_copy, compute)

    @pl.when(r == N_EXPERTS)
    def _():
        _zero_tail_blocks(nb, obuf, o_copy)


def _moe_down(h, w2, j, run_start, run_blocks):
    tm, tn = TM_FFN, TN_FFN2
    any_spec = pl.BlockSpec(memory_space=pl.ANY)
    grid_spec = pltpu.PrefetchScalarGridSpec(
        num_scalar_prefetch=2,
        grid=(D_MODEL // tn, N_RUNS),
        in_specs=[any_spec, any_spec],
        out_specs=any_spec,
        scratch_shapes=[
            pltpu.VMEM((2, D_FF, tn), F32),
            pltpu.VMEM((D_FF, tn), BF16),
            pltpu.VMEM((2, tm, D_FF), BF16), pltpu.VMEM((2, tm, 1, tn), F32),
            pltpu.SemaphoreType.DMA((2,)), pltpu.SemaphoreType.DMA((2,)), pltpu.SemaphoreType.DMA((2,)),
        ],
    )
    return pl.pallas_call(
        functools.partial(_moe_down_kernel, j=j),
        out_shape=jax.ShapeDtypeStruct((N_SLOTS, 1, D_MODEL), F32),
        grid_spec=grid_spec,
        compiler_params=_cparams(("arbitrary", "arbitrary"), VMEM_BUDGET),
        name="moe_down",
    )(run_start, run_blocks, h, w2)


def _row_copy(src_hbm, src_row, dst, dst_row, sem):
    return pltpu.make_async_copy(src_hbm.at[pl.ds(src_row, 1)], dst.at[pl.ds(dst_row, 1)], sem)


ROW_DMA_UNROLL = 8


def _for_each_row(n_rows, fn):
    def body(c, carry):
        for u in range(ROW_DMA_UNROLL):
            fn(c * ROW_DMA_UNROLL + u, u)
        return carry

    lax.fori_loop(0, n_rows // ROW_DMA_UNROLL, body, 0)


def _gather_kernel(tok_ref, u_hbm, o_ref, buf, flat, sem):
    i = pl.program_id(0)
    nb = pl.num_programs(0)
    tm = o_ref.shape[0]

    def issue(block, slot):
        _for_each_row(tm, lambda r, u: _row_copy(
            u_hbm, tok_ref[block * tm + r], buf.at[slot], r, sem.at[slot]).start(priority=u % 2))

    @pl.when(i == 0)
    def _():
        issue(0, 0)

    @pl.when(i + 1 < nb)
    def _():
        issue(i + 1, (i + 1) % 2)

    slot = i % 2
    _for_each_row(tm, lambda r, u: _row_copy(u_hbm, 0, buf.at[slot], r, sem.at[slot]).wait())
    flat[...] = buf[slot].reshape(flat.shape)
    o_ref[...] = flat[...].astype(BF16)


def _gather_rows(u, slot_token):
    tm = TM_FFN
    grid_spec = pltpu.PrefetchScalarGridSpec(
        num_scalar_prefetch=1,
        grid=(N_SLOTS // tm,),
        in_specs=[pl.BlockSpec(memory_space=pl.ANY)],
        out_specs=pl.BlockSpec((tm, D_MODEL), lambda i, tok: (i, 0)),
        scratch_shapes=[pltpu.VMEM((2, tm, 1, D_MODEL), F32), pltpu.VMEM((tm, D_MODEL), F32),
                        pltpu.SemaphoreType.DMA((2,))],
    )
    return pl.pallas_call(
        _gather_kernel,
        out_shape=jax.ShapeDtypeStruct((N_SLOTS, D_MODEL), BF16),
        grid_spec=grid_spec,
        compiler_params=_cparams(("arbitrary",), 32 * 2**20),
        name="moe_gather",
    )(slot_token, u)


def _combine_kernel(d_ref, y_hbm, x_ref, r_ref, mod_ref, o_ref, buf, flat, sem, *, row):
    i = pl.program_id(0)
    nb = pl.num_programs(0)
    tm = o_ref.shape[0]

    def issue(block, slot):
        def start_pair(r, u):
            t = block * tm + r
            _row_copy(y_hbm, d_ref[2 * t], buf.at[slot, 0], r, sem.at[slot]).start(priority=0)
            _row_copy(y_hbm, d_ref[2 * t + 1], buf.at[slot, 1], r, sem.at[slot]).start(priority=1)

        _for_each_row(tm, start_pair)

    @pl.when(i == 0)
    def _():
        issue(0, 0)

    @pl.when(i + 1 < nb)
    def _():
        issue(i + 1, (i + 1) % 2)

    slot = i % 2

    def wait_pair(r, u):
        _row_copy(y_hbm, 0, buf.at[slot, 0], r, sem.at[slot]).wait()
        _row_copy(y_hbm, 0, buf.at[slot, 1], r, sem.at[slot]).wait()

    _for_each_row(tm, wait_pair)
    for k in range(TOP_K):
        flat[k] = buf[slot, k].reshape(flat.shape[1:])
    g1 = r_ref[:, 2:3]
    g2 = r_ref[:, 3:4]
    y = flat[0] * g1 + flat[1] * g2
    o_ref[...] = x_ref[...] + mod_ref[0, row:row + 1, :] * y


def _moe_combine(x, y_slots, dest, route, mods, row):
    tm = TM_COMB
    grid_spec = pltpu.PrefetchScalarGridSpec(
        num_scalar_prefetch=1,
        grid=(M_TOK // tm,),
        in_specs=[
            pl.BlockSpec(memory_space=pl.ANY),
            pl.BlockSpec((tm, D_MODEL), lambda i, d: (i, 0)),
            pl.BlockSpec((tm, LANES), lambda i, d: (i, 0)),
            pl.BlockSpec((1, MOD_ROWS, D_MODEL), lambda i, d: (_group_of_tile(i, tm), 0, 0)),
        ],
        out_specs=pl.BlockSpec((tm, D_MODEL), lambda i, d: (i, 0)),
        scratch_shapes=[pltpu.VMEM((2, TOP_K, tm, 1, D_MODEL), F32), pltpu.VMEM((TOP_K, tm, D_MODEL), F32),
                        pltpu.SemaphoreType.DMA((2,))],
    )
    return pl.pallas_call(
        functools.partial(_combine_kernel, row=row),
        out_shape=jax.ShapeDtypeStruct((M_TOK, D_MODEL), F32),
        grid_spec=grid_spec,
        compiler_params=_cparams(("arbitrary",), 32 * 2**20),
        name="moe_combine",
    )(dest.reshape(-1), y_slots, x, route, mods)


def _moe_plan(route):
    tm = TM_FFN
    expert = route[:, :TOP_K].astype(jnp.int32).reshape(-1)
    onehot = (expert[:, None] == jnp.arange(N_EXPERTS, dtype=jnp.int32)[None, :]).astype(jnp.int32)
    csum = jnp.cumsum(onehot, axis=0)
    rank = jnp.take_along_axis(csum, expert[:, None], axis=1)[:, 0] - 1
    counts = csum[-1]
    padded = (counts + tm - 1) // tm * tm
    pend = jnp.cumsum(padded)
    pstart = pend - padded
    dest = pstart[expert] + rank
    token = jnp.arange(M_TOK * TOP_K, dtype=jnp.int32) // TOP_K
    slot_token = jnp.zeros((N_SLOTS,), jnp.int32).at[dest].set(token)
    n_valid = pend[-1:] // tm
    run_start = jnp.concatenate([pstart // tm, n_valid]).astype(jnp.int32)
    run_blocks = jnp.concatenate([padded // tm, N_SLOTS_BLOCKS - n_valid]).astype(jnp.int32)
    return slot_token, dest.astype(jnp.int32).reshape(M_TOK, TOP_K), run_start, run_blocks


def kernel(x_prompt, x_sample, cache_k, cache_v, c, c_ctx, ada_w, ada_b, norm1_g, norm2_g, w_in, qn_g, kn_g, pool_w, pool_scale, rel_bias, w_bp, w_ba, w_out, ffn_w1, ffn_w3, ffn_w2, router_w, router_b, moe_w1, moe_w3, moe_w2):
    x = jnp.concatenate([x_prompt.reshape(M_CTX, D_MODEL), x_sample.reshape(M_LAT, D_MODEL)], axis=0)
    cond8 = jnp.zeros((MOD_ROWS, D_MODEL), F32).at[0].set(c_ctx).at[1:1 + DEC_BATCH].set(c)
    mods_all = _adaln_all(cond8, ada_w, ada_b)
    mods_all = mods_all[:, :N_GROUPS].reshape(DEPTH, N_GROUPS, 6, D_MODEL)
    mods_all = jnp.pad(mods_all, ((0, 0), (0, 0), (0, MOD_ROWS - 6), (0, 0)))
    t2_all = _bias_tables(rel_bias)

    new_k = jnp.zeros((BATCH, DEPTH, N_HEADS, SEQ, HEAD_DIM), F32)
    new_v = jnp.zeros((BATCH, DEPTH, N_HEADS, SEQ, HEAD_DIM), F32)
    for l in range(DEPTH):
        j = l // 2
        mods = mods_all[l]
        u = _norm_mod(x, norm1_g[l], mods, 0)
        proj = _in_proj(u, w_in, l)
        pool_out = _pool_mixer(proj, pool_w[l], pool_scale[l])
        attn_ctx, new_k, new_v = _ctx_attention(proj, qn_g[l], kn_g[l], l, new_k, new_v)
        attn_lat = _lat_attention(proj, cache_k, cache_v, t2_all, qn_g[l], kn_g[l], l)
        merged = _mix_merge(pool_out, attn_ctx, attn_lat, w_bp, w_ba, proj, l)
        x = _out_proj_residual(merged, w_out, l, x, mods, 2)
        if l % 2 == 0:
            u2 = _norm_mod(x, norm2_g[l], mods, 3)
            h = _ffn_up(u2, ffn_w1, ffn_w3, j)
            x = _ffn_down(h, ffn_w2, j, x, mods, 5)
        else:
            u2, route = _norm_route(x, norm2_g[l], mods, 3, router_w[j], router_b[j])
            slot_token, dest, run_start, run_blocks = _moe_plan(route)
            xs = _gather_rows(u2, slot_token)
            h = _moe_up(xs, moe_w1, moe_w3, j, run_start, run_blocks)
            y_slots = _moe_down(h, moe_w2, j, run_start, run_blocks)
            x = _moe_combine(x, y_slots, dest, route, mods, 5)

    y_prompt = x[:M_CTX].reshape(BATCH, SEQ, D_MODEL)
    y_sample = x[M_CTX:].reshape(DEC_BATCH, DEC_SEQ, D_MODEL)
    return (y_prompt, y_sample, new_k, new_v)
```

```python
import functools

import jax
import jax.numpy as jnp
from jax import lax
from jax.experimental import pallas as pl
from jax.experimental.pallas import tpu as pltpu

F32 = jnp.float32
BF16 = jnp.bfloat16

D_MODEL = 2048
BATCH = 32
SEQ = 256
DEPTH = 4
DEC_BATCH = 2
DEC_SEQ = 2048
PAST_LEN = 512
GRID_W = 64
GRID_ROWS = DEC_SEQ // GRID_W
POOL_WIDTH = D_MODEL // 2
N_POOL_GROUPS = 4
POOL_GROUP_W = POOL_WIDTH // N_POOL_GROUPS
POOL_WINDOWS = (2, 4, 8, 16)
ATTN_WIDTH = D_MODEL // 2
HEAD_DIM = 64
N_HEADS = ATTN_WIDTH // HEAD_DIM
WIN_ROWS = 8
WIN_COLS = 16
D_FF = ((8 * D_MODEL // 3 + 255) // 256) * 256
N_EXPERTS = 8
TOP_K = 2
PROJ_WIDTH = POOL_WIDTH + 3 * ATTN_WIDTH + 2 * D_MODEL
RMS_EPS = 1e-6
NEG_INF = -1e30
ATTN_SCALE = HEAD_DIM ** -0.5

M_CTX = BATCH * SEQ
M_LAT = DEC_BATCH * DEC_SEQ
M_TOK = M_CTX + M_LAT
N_GROUPS = 1 + DEC_BATCH
MOD_ROWS = 8

LANES = 128
V7X_VMEM_BYTES = 64 * 1024 * 1024
VMEM_BUDGET = V7X_VMEM_BYTES - 8 * 1024 * 1024

CAST_ROWS = 256

TM_NORM = 512
TM_IN, TN_IN = 1024, 1024
TM_MIX, TN_MIX = 512, 1024
TM_OUT, TN_OUT = 512, 1024
TM_FFN, TN_FFN1, TN_FFN2 = 512, 512, 512
TM_FFN_UP_DENSE = 1024
TM_COMB = 256
POOL_TILE = 256
POOL_HALO = 8
LAT_QROWS = 8
LAT_QBLK = LAT_QROWS * GRID_W
LAT_KROWS = 16
LAT_KWIN = LAT_KROWS * GRID_W
N_SLOTS_BLOCKS = (M_TOK * TOP_K) // TM_FFN + N_EXPERTS
N_SLOTS = N_SLOTS_BLOCKS * TM_FFN


def _cparams(semantics, vmem_bytes):
    return pltpu.CompilerParams(dimension_semantics=semantics, vmem_limit_bytes=int(vmem_bytes))


def _group_of_tile(m, tm):
    start = m * tm
    return jnp.where(start < M_CTX, 0, 1 + (start - M_CTX) // DEC_SEQ)


def _sigmoid(z):
    return 1.0 / (1.0 + jnp.exp(-z))


def _silu(z):
    return z / (1.0 + jnp.exp(-z))


def _rms(x):
    return x * lax.rsqrt(jnp.mean(x * x, axis=-1, keepdims=True) + RMS_EPS)


def _cast_weight(w_ref, scr_ref, k_rows):
    def body(i, carry):
        r = pl.multiple_of(i * CAST_ROWS, CAST_ROWS)
        scr_ref[pl.ds(r, CAST_ROWS), :] = w_ref[pl.ds(r, CAST_ROWS), :].astype(BF16)
        return carry

    lax.fori_loop(0, k_rows // CAST_ROWS, body, 0)


def _ada_kernel(c_ref, w_ref, b_ref, o_ref):
    s = _silu(c_ref[...]).astype(BF16)
    w = w_ref[0].astype(BF16)
    o_ref[0] = jnp.dot(s, w, preferred_element_type=F32) + b_ref[0]


def _adaln_all(cond8, ada_w, ada_b):
    tn = 1024
    n_out = 6 * D_MODEL
    return pl.pallas_call(
        _ada_kernel,
        out_shape=jax.ShapeDtypeStruct((DEPTH, MOD_ROWS, n_out), F32),
        grid=(DEPTH, n_out // tn),
        in_specs=[
            pl.BlockSpec((MOD_ROWS, D_MODEL), lambda l, n: (0, 0)),
            pl.BlockSpec((1, D_MODEL, tn), lambda l, n: (l, 0, n)),
            pl.BlockSpec((1, 1, tn), lambda l, n: (l, 0, n)),
        ],
        out_specs=pl.BlockSpec((1, MOD_ROWS, tn), lambda l, n: (l, 0, n)),
        compiler_params=_cparams(("arbitrary", "arbitrary"), 40 * 2**20),
        name="adaln",
    )(cond8, ada_w, ada_b.reshape(DEPTH, 1, n_out))


def _token_specs(x, tm, width, col_of):
    if not isinstance(x, tuple):
        return [pl.BlockSpec((tm, width), lambda *g: col_of(*g))], [x]
    n_ctx = M_CTX // tm

    def ctx_map(*g):
        m, c = col_of(*g)
        return (jnp.minimum(m, n_ctx - 1), c)

    def lat_map(*g):
        m, c = col_of(*g)
        return (jnp.maximum(m - n_ctx, 0), c)

    return [pl.BlockSpec((tm, width), ctx_map), pl.BlockSpec((tm, width), lat_map)], list(x)


def _pick_tile(m, tm, refs):
    if len(refs) == 1:
        return refs[0][...]
    return jnp.where(m < M_CTX // tm, refs[0][...], refs[1][...])


def _norm_mod_kernel_x(*refs, row):
    x_refs, (g_ref, mod_ref, o_ref) = refs[:-3], refs[-3:]
    y = _rms(_pick_tile(pl.program_id(0), TM_NORM, x_refs)) * g_ref[...]
    sh = mod_ref[0, row:row + 1, :]
    sc = mod_ref[0, row + 1:row + 2, :]
    o_ref[...] = (y * (1.0 + sc) + sh).astype(o_ref.dtype)


def _norm_mod(x, g, mods, row):
    tm = TM_NORM
    x_specs, x_args = _token_specs(x, tm, D_MODEL, lambda m: (m, 0))
    return pl.pallas_call(
        functools.partial(_norm_mod_kernel_x, row=row),
        out_shape=jax.ShapeDtypeStruct((M_TOK, D_MODEL), BF16),
        grid=(M_TOK // tm,),
        in_specs=x_specs + [
            pl.BlockSpec((1, D_MODEL), lambda m: (0, 0)),
            pl.BlockSpec((1, MOD_ROWS, D_MODEL), lambda m: (_group_of_tile(m, tm), 0, 0)),
        ],
        out_specs=pl.BlockSpec((tm, D_MODEL), lambda m: (m, 0)),
        compiler_params=_cparams(("parallel",), 44 * 2**20),
        name="norm_mod",
    )(*x_args, g.reshape(1, D_MODEL), mods)


def _norm_route_kernel(x_ref, g_ref, mod_ref, rwh_ref, rwl_ref, rb_ref, u_ref, r_ref, *, row):
    y = _rms(x_ref[...]) * g_ref[...]
    sh = mod_ref[0, row:row + 1, :]
    sc = mod_ref[0, row + 1:row + 2, :]
    u = y * (1.0 + sc) + sh
    u_ref[...] = u.reshape(u_ref.shape)
    u_hi = u.astype(BF16)
    u_lo = (u - u_hi.astype(F32)).astype(BF16)
    logits = (jnp.dot(u_hi, rwh_ref[...], preferred_element_type=F32)
              + jnp.dot(u_hi, rwl_ref[...], preferred_element_type=F32)
              + jnp.dot(u_lo, rwh_ref[...], preferred_element_type=F32)) + rb_ref[...]
    lane = lax.broadcasted_iota(jnp.int32, logits.shape, 1).astype(F32)
    big = float(LANES)
    m1 = jnp.max(logits, axis=-1, keepdims=True)
    i1 = jnp.min(jnp.where(logits == m1, lane, big), axis=-1, keepdims=True)
    rest = jnp.where(lane == i1, -jnp.inf, logits)
    m2 = jnp.max(rest, axis=-1, keepdims=True)
    i2 = jnp.min(jnp.where(rest == m2, lane, big), axis=-1, keepdims=True)
    e2 = jnp.exp(m2 - m1)
    den = 1.0 + e2
    r_ref[...] = jnp.where(lane == 0.0, i1,
                           jnp.where(lane == 1.0, i2,
                                     jnp.where(lane == 2.0, 1.0 / den,
                                               jnp.where(lane == 3.0, e2 / den, 0.0))))


def _norm_route(x, g, mods, row, router_w, router_b):
    tm = TM_NORM
    rw = jnp.zeros((D_MODEL, LANES), F32).at[:, :N_EXPERTS].set(router_w)
    rw_hi = rw.astype(BF16)
    rw_lo = (rw - rw_hi.astype(F32)).astype(BF16)
    rb = jnp.full((1, LANES), -jnp.inf, F32).at[0, :N_EXPERTS].set(router_b)
    return pl.pallas_call(
        functools.partial(_norm_route_kernel, row=row),
        out_shape=(jax.ShapeDtypeStruct((M_TOK, 1, D_MODEL), F32),
                   jax.ShapeDtypeStruct((M_TOK, LANES), F32)),
        grid=(M_TOK // tm,),
        in_specs=[
            pl.BlockSpec((tm, D_MODEL), lambda m: (m, 0)),
            pl.BlockSpec((1, D_MODEL), lambda m: (0, 0)),
            pl.BlockSpec((1, MOD_ROWS, D_MODEL), lambda m: (_group_of_tile(m, tm), 0, 0)),
            pl.BlockSpec((D_MODEL, LANES), lambda m: (0, 0)),
            pl.BlockSpec((D_MODEL, LANES), lambda m: (0, 0)),
            pl.BlockSpec((1, LANES), lambda m: (0, 0)),
        ],
        out_specs=(pl.BlockSpec((tm, 1, D_MODEL), lambda m: (m, 0, 0)),
                   pl.BlockSpec((tm, LANES), lambda m: (m, 0))),
        compiler_params=_cparams(("parallel",), 40 * 2**20),
        name="norm_route",
    )(x, g.reshape(1, D_MODEL), mods, rw_hi, rw_lo, rb)


def _in_proj_kernel(a_ref, w_ref, o_ref, wscr):
    @pl.when(pl.program_id(1) == 0)
    def _():
        _cast_weight(w_ref.at[0], wscr, D_MODEL)

    o_ref[...] = jnp.dot(a_ref[...], wscr[...], preferred_element_type=F32)


def _in_proj(u, w_in, layer):
    tm, tn = TM_IN, TN_IN
    return pl.pallas_call(
        _in_proj_kernel,
        out_shape=jax.ShapeDtypeStruct((M_TOK, PROJ_WIDTH), F32),
        grid=(PROJ_WIDTH // tn, M_TOK // tm),
        in_specs=[
            pl.BlockSpec((tm, D_MODEL), lambda n, m: (m, 0)),
            pl.BlockSpec((1, D_MODEL, tn), lambda n, m: (layer, 0, n)),
        ],
        out_specs=pl.BlockSpec((tm, tn), lambda n, m: (m, n)),
        scratch_shapes=[pltpu.VMEM((D_MODEL, tn), BF16)],
        compiler_params=_cparams(("arbitrary", "arbitrary"), VMEM_BUDGET),
        name="in_proj",
    )(u, w_in)


def _pool_kernel(xm_ref, xp_ref, xn_ref, pw_ref, ps_ref, o_ref, pad):
    i = pl.program_id(0)
    n_ctx_tiles = M_CTX // POOL_TILE
    tiles_per_lat = DEC_SEQ // POOL_TILE
    is_lat = i >= n_ctx_tiles
    j = (i - n_ctx_tiles) % tiles_per_lat
    has_prev = jnp.logical_and(is_lat, j != 0)
    has_next = jnp.logical_and(is_lat, j != tiles_per_lat - 1)
    pad[0:POOL_HALO, :] = jnp.where(has_prev, xp_ref[...], 0.0)
    pad[POOL_HALO:POOL_HALO + POOL_TILE, :] = xm_ref[...]
    pad[POOL_HALO + POOL_TILE:, :] = jnp.where(has_next, xn_ref[...], 0.0)
    off = jnp.where(is_lat, j * POOL_TILE, 0)
    seq_len = jnp.where(is_lat, DEC_SEQ, SEQ)
    pos = lax.broadcasted_iota(jnp.int32, (POOL_TILE, 1), 0) + off
    for g, win in enumerate(POOL_WINDOWS):
        lanes = slice(g * POOL_GROUP_W, (g + 1) * POOL_GROUP_W)
        back, fwd = win // 2, win - win // 2
        acc = pad[pl.ds(POOL_HALO - back, POOL_TILE), lanes]
        for d in range(-back + 1, fwd):
            acc = acc + pad[pl.ds(POOL_HALO + d, POOL_TILE), lanes]
        cnt = (jnp.minimum(pos + fwd, seq_len) - jnp.maximum(pos - back, 0)).astype(F32)
        pooled = (acc / cnt - xm_ref[:, lanes]).astype(BF16)
        mixed = jnp.dot(pooled, pw_ref[g].astype(BF16), preferred_element_type=F32)
        o_ref[:, lanes] = (mixed * ps_ref[:, lanes]).astype(BF16)


def _pool_mixer(proj, pool_w, pool_scale):
    t, h = POOL_TILE, POOL_HALO
    hb = t // h
    last_halo = M_TOK // h - 1
    return pl.pallas_call(
        _pool_kernel,
        out_shape=jax.ShapeDtypeStruct((M_TOK, POOL_WIDTH), BF16),
        grid=(M_TOK // t,),
        in_specs=[
            pl.BlockSpec((t, POOL_WIDTH), lambda i: (i, 0)),
            pl.BlockSpec((h, POOL_WIDTH), lambda i: (jnp.maximum(i * hb - 1, 0), 0)),
            pl.BlockSpec((h, POOL_WIDTH), lambda i: (jnp.minimum((i + 1) * hb, last_halo), 0)),
            pl.BlockSpec((N_POOL_GROUPS, POOL_GROUP_W, POOL_GROUP_W), lambda i: (0, 0, 0)),
            pl.BlockSpec((1, POOL_WIDTH), lambda i: (0, 0)),
        ],
        out_specs=pl.BlockSpec((t, POOL_WIDTH), lambda i: (i, 0)),
        scratch_shapes=[pltpu.VMEM((t + 2 * h, POOL_WIDTH), F32)],
        compiler_params=_cparams(("parallel",), 24 * 2**20),
        name="pool_mixer",
    )(proj, proj, proj, pool_w, pool_scale.reshape(1, POOL_WIDTH))


PAIR_W = 2 * HEAD_DIM


def _pair_group_ones():
    r = lax.broadcasted_iota(jnp.int32, (PAIR_W, PAIR_W), 0) // HEAD_DIM
    c = lax.broadcasted_iota(jnp.int32, (PAIR_W, PAIR_W), 1) // HEAD_DIM
    return jnp.where(r == c, 1.0, 0.0).astype(BF16)


def _head_masks():
    lane = lax.broadcasted_iota(jnp.int32, (1, PAIR_W), 1)
    return (lane < HEAD_DIM, lane >= HEAD_DIM)


def _rms_pair(x, ones):
    y2 = x * x
    hi = y2.astype(BF16)
    lo = (y2 - hi.astype(F32)).astype(BF16)
    ss = jnp.dot(hi, ones, preferred_element_type=F32) + jnp.dot(lo, ones, preferred_element_type=F32)
    return x * lax.rsqrt(ss * (1.0 / HEAD_DIM) + RMS_EPS)


def _softmax_rows(s):
    e = jnp.exp(s - jnp.max(s, axis=-1, keepdims=True))
    return e * (1.0 / jnp.sum(e, axis=-1, keepdims=True))


_NT_DIMS = (((1,), (1,)), ((), ()))


def _ctx_attn_kernel(q_ref, k_ref, v_ref, qg_ref, kg_ref, _ck_in, _cv_in, o_ref, ko_ref, vo_ref):
    qg = qg_ref[...] * ATTN_SCALE
    kg = kg_ref[...]
    ones = _pair_group_ones()
    masks = _head_masks()
    for p in range(N_HEADS // 2):
        sl = slice(p * PAIR_W, (p + 1) * PAIR_W)
        kn = _rms_pair(k_ref[:, sl], ones) * kg
        v = v_ref[:, sl]
        for hh in range(2):
            lanes = slice(hh * HEAD_DIM, (hh + 1) * HEAD_DIM)
            ko_ref[0, 0, 2 * p + hh] = kn[:, lanes]
            vo_ref[0, 0, 2 * p + hh] = v[:, lanes]
        qn = _rms_pair(q_ref[:, sl], ones) * qg
        kb = kn.astype(BF16)
        o = jnp.zeros((SEQ, PAIR_W), F32)
        for hh in range(2):
            qh = jnp.where(masks[hh], qn, 0.0).astype(BF16)
            s = lax.dot_general(qh, kb, _NT_DIMS, preferred_element_type=F32)
            pr = _softmax_rows(s).astype(BF16)
            vh = jnp.where(masks[hh], v, 0.0).astype(BF16)
            o = o + jnp.dot(pr, vh, preferred_element_type=F32)
        o_ref[:, sl] = o.astype(BF16)


def _pair_gain(g):
    return jnp.concatenate([g, g]).reshape(1, PAIR_W)


def _ctx_attention(proj, qn_g, kn_g, layer, new_k, new_v):
    q_blk = POOL_WIDTH // ATTN_WIDTH
    cache_shape = jax.ShapeDtypeStruct((BATCH, DEPTH, N_HEADS, SEQ, HEAD_DIM), F32)
    cache_spec = pl.BlockSpec((1, 1, N_HEADS, SEQ, HEAD_DIM), lambda b: (b, layer, 0, 0, 0))
    return pl.pallas_call(
        _ctx_attn_kernel,
        out_shape=(jax.ShapeDtypeStruct((M_CTX, ATTN_WIDTH), BF16), cache_shape, cache_shape),
        grid=(BATCH,),
        in_specs=[
            pl.BlockSpec((SEQ, ATTN_WIDTH), lambda b: (b, q_blk)),
            pl.BlockSpec((SEQ, ATTN_WIDTH), lambda b: (b, q_blk + 1)),
            pl.BlockSpec((SEQ, ATTN_WIDTH), lambda b: (b, q_blk + 2)),
            pl.BlockSpec((1, PAIR_W), lambda b: (0, 0)),
            pl.BlockSpec((1, PAIR_W), lambda b: (0, 0)),
            pl.BlockSpec(memory_space=pl.ANY),
            pl.BlockSpec(memory_space=pl.ANY),
        ],
        out_specs=(pl.BlockSpec((SEQ, ATTN_WIDTH), lambda b: (b, 0)), cache_spec, cache_spec),
        input_output_aliases={5: 1, 6: 2},
        compiler_params=_cparams(("arbitrary",), 40 * 2**20),
        name="ctx_attention",
    )(proj, proj, proj, _pair_gain(qn_g), _pair_gain(kn_g), new_k, new_v)


def _lat_attn_kernel(q_ref, k_ref, v_ref, ck_ref, cv_ref, t2_ref, qg_ref, kg_ref, o_ref,
                     kscr, vscr, ckscr, cvscr, sscr):
    blk = pl.program_id(2)

    ones = _pair_group_ones()
    masks = _head_masks()

    @pl.when(blk == 0)
    def _():
        kscr[...] = (_rms_pair(k_ref[...], ones) * kg_ref[...]).astype(BF16)
        ckscr[...] = jnp.concatenate([ck_ref[0, 0, 0], ck_ref[0, 0, 1]], axis=-1).astype(BF16)
        v = v_ref[...]
        cv = jnp.concatenate([cv_ref[0, 0, 0], cv_ref[0, 0, 1]], axis=-1)
        for hh in range(2):
            vscr[hh] = jnp.where(masks[hh], v, 0.0).astype(BF16)
            cvscr[hh] = jnp.where(masks[hh], cv, 0.0).astype(BF16)

    r0 = blk * LAT_QROWS
    w0 = jnp.clip(r0 - WIN_ROWS // 2, 0, GRID_ROWS - LAT_KROWS)
    kstart = pl.multiple_of(w0 * GRID_W, 256)
    lane_lo = lax.broadcasted_iota(jnp.int32, (GRID_W, 2 * GRID_W), 1) < GRID_W
    qn = _rms_pair(q_ref[...], ones) * (qg_ref[...] * ATTN_SCALE)
    kw = kscr[pl.ds(kstart, LAT_KWIN), :]
    ckb = ckscr[...]
    o = jnp.zeros((LAT_QBLK, PAIR_W), F32)
    for hh in range(2):
        qh = jnp.where(masks[hh], qn, 0.0).astype(BF16)
        s = lax.dot_general(qh, kw, _NT_DIMS, preferred_element_type=F32)
        for qi in range(LAT_QROWS):
            qr = r0 + qi
            rs = jnp.clip(qr - WIN_ROWS // 2, 0, GRID_ROWS - WIN_ROWS)
            for jp in range(LAT_KROWS // 2):
                kr0 = w0 + 2 * jp
                idx = jnp.clip(kr0 - qr + WIN_ROWS, 0, 2 * WIN_ROWS - 1)
                ok0 = jnp.logical_and(kr0 >= rs, kr0 < rs + WIN_ROWS).astype(jnp.int32)
                ok1 = jnp.logical_and(kr0 + 1 >= rs, kr0 + 1 < rs + WIN_ROWS).astype(jnp.int32)
                ok = jnp.where(lane_lo, ok0, ok1) > 0
                tile = jnp.where(ok, t2_ref[0, hh, idx], NEG_INF)
                rows = slice(qi * GRID_W, (qi + 1) * GRID_W)
                cols = slice(jp * 2 * GRID_W, (jp + 1) * 2 * GRID_W)
                sscr[rows, cols] = s[rows, cols] + tile
        s_loc = sscr[...]
        s_ctx = lax.dot_general(qh, ckb, _NT_DIMS, preferred_element_type=F32)
        m = jnp.maximum(jnp.max(s_loc, axis=-1, keepdims=True), jnp.max(s_ctx, axis=-1, keepdims=True))
        e_loc = jnp.exp(s_loc - m)
        e_ctx = jnp.exp(s_ctx - m)
        inv = 1.0 / (jnp.sum(e_loc, axis=-1, keepdims=True) + jnp.sum(e_ctx, axis=-1, keepdims=True))
        vh = vscr[hh, pl.ds(kstart, LAT_KWIN), :]
        o = o + inv * (jnp.dot(e_loc.astype(BF16), vh, preferred_element_type=F32)
                       + jnp.dot(e_ctx.astype(BF16), cvscr[hh], preferred_element_type=F32))
    o_ref[...] = o.astype(BF16)


def _bias_tables(rel_bias):
    col = jnp.arange(GRID_W)
    col_start = jnp.clip(col - WIN_COLS // 2, 0, GRID_W - WIN_COLS)
    col_mask = (col[None, :] >= col_start[:, None]) & (col[None, :] < col_start[:, None] + WIN_COLS)
    dc_idx = jnp.clip(col[None, :] - col[:, None], -(WIN_COLS - 1), WIN_COLS - 1) + (WIN_COLS - 1)
    onehot = (dc_idx[:, :, None] == jnp.arange(2 * WIN_COLS - 1)[None, None, :]).astype(F32)
    t = jnp.einsum('lhrc,qkc->lhrqk', rel_bias.astype(F32), onehot, precision=lax.Precision.HIGHEST)
    t = jnp.where(col_mask, t, NEG_INF)
    neg = jnp.full(t.shape[:2] + (1,) + t.shape[3:], NEG_INF, F32)
    tp = jnp.concatenate([neg, t, neg], axis=2)
    return jnp.concatenate([tp[:, :, :2 * WIN_ROWS], tp[:, :, 1:2 * WIN_ROWS + 1]], axis=-1)


def _lat_attention(proj, cache_k, cache_v, t2, qn_g, kn_g, layer):
    hp_w = 2 * HEAD_DIM
    q_col = POOL_WIDTH // hp_w
    k_col = (POOL_WIDTH + ATTN_WIDTH) // hp_w
    v_col = (POOL_WIDTH + 2 * ATTN_WIDTH) // hp_w
    lat_row_seq = M_CTX // DEC_SEQ
    lat_row_blk = M_CTX // LAT_QBLK
    n_blk = DEC_SEQ // LAT_QBLK
    return pl.pallas_call(
        _lat_attn_kernel,
        out_shape=jax.ShapeDtypeStruct((M_LAT, ATTN_WIDTH), BF16),
        grid=(DEC_BATCH, N_HEADS // 2, n_blk),
        in_specs=[
            pl.BlockSpec((LAT_QBLK, hp_w), lambda b, hp, blk: (lat_row_blk + b * n_blk + blk, q_col + hp)),
            pl.BlockSpec((DEC_SEQ, hp_w), lambda b, hp, blk: (lat_row_seq + b, k_col + hp)),
            pl.BlockSpec((DEC_SEQ, hp_w), lambda b, hp, blk: (lat_row_seq + b, v_col + hp)),
            pl.BlockSpec((1, 1, 2, PAST_LEN, HEAD_DIM), lambda b, hp, blk: (b, layer, hp, 0, 0)),
            pl.BlockSpec((1, 1, 2, PAST_LEN, HEAD_DIM), lambda b, hp, blk: (b, layer, hp, 0, 0)),
            pl.BlockSpec((1, 2, 2 * WIN_ROWS, GRID_W, 2 * GRID_W), lambda b, hp, blk: (layer, hp, 0, 0, 0)),
            pl.BlockSpec((1, PAIR_W), lambda b, hp, blk: (0, 0)),
            pl.BlockSpec((1, PAIR_W), lambda b, hp, blk: (0, 0)),
        ],
        out_specs=pl.BlockSpec((LAT_QBLK, hp_w), lambda b, hp, blk: (b * n_blk + blk, hp)),
        scratch_shapes=[
            pltpu.VMEM((DEC_SEQ, PAIR_W), BF16),
            pltpu.VMEM((2, DEC_SEQ, PAIR_W), BF16),
            pltpu.VMEM((PAST_LEN, PAIR_W), BF16),
            pltpu.VMEM((2, PAST_LEN, PAIR_W), BF16),
            pltpu.VMEM((LAT_QBLK, LAT_KWIN), F32),
        ],
        compiler_params=_cparams(("arbitrary", "arbitrary", "arbitrary"), 48 * 2**20),
        name="lat_attention",
    )(proj, proj, proj, cache_k, cache_v, t2, _pair_gain(qn_g), _pair_gain(kn_g))


def _mix_kernel(p_ref, ac_ref, al_ref, wbp_ref, wba_ref, gp_ref, ga_ref, o_ref, sbp, sba):
    m = pl.program_id(1)

    @pl.when(m == 0)
    def _():
        _cast_weight(wbp_ref.at[0], sbp, POOL_WIDTH)
        _cast_weight(wba_ref.at[0], sba, ATTN_WIDTH)

    a = jnp.where(m < M_CTX // TM_MIX, ac_ref[...], al_ref[...])
    yp = jnp.dot(p_ref[...], sbp[...], preferred_element_type=F32)
    ya = jnp.dot(a, sba[...], preferred_element_type=F32)
    o_ref[...] = (_sigmoid(gp_ref[...]) * yp + _sigmoid(ga_ref[...]) * ya).astype(BF16)


def _mix_merge(pool_out, attn_ctx, attn_lat, w_bp, w_ba, proj, layer):
    tm, tn = TM_MIX, TN_MIX
    gp_col = (POOL_WIDTH + 3 * ATTN_WIDTH) // tn
    ga_col = gp_col + D_MODEL // tn
    n_ctx = M_CTX // tm
    return pl.pallas_call(
        _mix_kernel,
        out_shape=jax.ShapeDtypeStruct((M_TOK, D_MODEL), BF16),
        grid=(D_MODEL // tn, M_TOK // tm),
        in_specs=[
            pl.BlockSpec((tm, POOL_WIDTH), lambda n, m: (m, 0)),
            pl.BlockSpec((tm, ATTN_WIDTH), lambda n, m: (jnp.minimum(m, n_ctx - 1), 0)),
            pl.BlockSpec((tm, ATTN_WIDTH), lambda n, m: (jnp.maximum(m - n_ctx, 0), 0)),
            pl.BlockSpec((1, POOL_WIDTH, tn), lambda n, m: (layer, 0, n)),
            pl.BlockSpec((1, ATTN_WIDTH, tn), lambda n, m: (layer, 0, n)),
            pl.BlockSpec((tm, tn), lambda n, m: (m, gp_col + n)),
            pl.BlockSpec((tm, tn), lambda n, m: (m, ga_col + n)),
        ],
        out_specs=pl.BlockSpec((tm, tn), lambda n, m: (m, n)),
        scratch_shapes=[pltpu.VMEM((POOL_WIDTH, tn), BF16), pltpu.VMEM((ATTN_WIDTH, tn), BF16)],
        compiler_params=_cparams(("arbitrary", "arbitrary"), 48 * 2**20),
        name="mix_merge",
    )(pool_out, attn_ctx, attn_lat, w_bp, w_ba, proj, proj)


def _out_proj_kernel(a_ref, w_ref, mod_ref, *rest, row):
    x_refs, (o_ref, wscr) = rest[:-2], rest[-2:]
    m = pl.program_id(1)

    @pl.when(m == 0)
    def _():
        _cast_weight(w_ref.at[0], wscr, D_MODEL)

    y = jnp.dot(a_ref[...], wscr[...], preferred_element_type=F32)
    o_ref[...] = _pick_tile(m, TM_OUT, x_refs) + mod_ref[0, row:row + 1, :] * y


def _out_proj_residual(merged, w_out, layer, x, mods, row):
    tm, tn = TM_OUT, TN_OUT
    x_specs, x_args = _token_specs(x, tm, tn, lambda n, m: (m, n))
    return pl.pallas_call(
        functools.partial(_out_proj_kernel, row=row),
        out_shape=jax.ShapeDtypeStruct((M_TOK, D_MODEL), F32),
        grid=(D_MODEL // tn, M_TOK // tm),
        in_specs=[
            pl.BlockSpec((tm, D_MODEL), lambda n, m: (m, 0)),
            pl.BlockSpec((1, D_MODEL, tn), lambda n, m: (layer, 0, n)),
            pl.BlockSpec((1, MOD_ROWS, tn), lambda n, m: (_group_of_tile(m, tm), 0, n)),
        ] + x_specs,
        out_specs=pl.BlockSpec((tm, tn), lambda n, m: (m, n)),
        scratch_shapes=[pltpu.VMEM((D_MODEL, tn), BF16)],
        compiler_params=_cparams(("arbitrary", "arbitrary"), VMEM_BUDGET),
        name="out_proj",
    )(merged, w_out, mods, *x_args)


def _ffn_up_kernel(a_ref, w1_ref, w3_ref, o_ref, s1, s3):
    @pl.when(pl.program_id(1) == 0)
    def _():
        _cast_weight(w1_ref.at[0], s1, D_MODEL)
        _cast_weight(w3_ref.at[0], s3, D_MODEL)

    a = a_ref[...]
    h1 = jnp.dot(a, s1[...], preferred_element_type=F32)
    h3 = jnp.dot(a, s3[...], preferred_element_type=F32)
    o_ref[...] = (_silu(h1) * h3).astype(BF16)


def _ffn_up(a, w1, w3, j):
    tm, tn = TM_FFN_UP_DENSE, TN_FFN1
    return pl.pallas_call(
        _ffn_up_kernel,
        out_shape=jax.ShapeDtypeStruct((M_TOK, D_FF), BF16),
        grid=(D_FF // tn, M_TOK // tm),
        in_specs=[
            pl.BlockSpec((tm, D_MODEL), lambda n, m: (m, 0)),
            pl.BlockSpec((1, D_MODEL, tn), lambda n, m: (j, 0, n)),
            pl.BlockSpec((1, D_MODEL, tn), lambda n, m: (j, 0, n)),
        ],
        out_specs=pl.BlockSpec((tm, tn), lambda n, m: (m, n)),
        scratch_shapes=[pltpu.VMEM((D_MODEL, tn), BF16), pltpu.VMEM((D_MODEL, tn), BF16)],
        compiler_params=_cparams(("arbitrary", "arbitrary"), 48 * 2**20),
        name="ffn_up",
    )(a, w1, w3)


def _ffn_down_kernel(a_ref, w_ref, x_ref, mod_ref, o_ref, wscr, *, row):
    @pl.when(pl.program_id(1) == 0)
    def _():
        _cast_weight(w_ref.at[0], wscr, D_FF)

    y = jnp.dot(a_ref[...], wscr[...], preferred_element_type=F32)
    o_ref[...] = x_ref[...] + mod_ref[0, row:row + 1, :] * y


def _ffn_down(h, w2, j, x, mods, row):
    tm, tn = TM_FFN, TN_FFN2
    return pl.pallas_call(
        functools.partial(_ffn_down_kernel, row=row),
        out_shape=jax.ShapeDtypeStruct((M_TOK, D_MODEL), F32),
        grid=(D_MODEL // tn, M_TOK // tm),
        in_specs=[
            pl.BlockSpec((tm, D_FF), lambda n, m: (m, 0)),
            pl.BlockSpec((1, D_FF, tn), lambda n, m: (j, 0, n)),
            pl.BlockSpec((tm, tn), lambda n, m: (m, n)),
            pl.BlockSpec((1, MOD_ROWS, tn), lambda n, m: (_group_of_tile(m, tm), 0, n)),
        ],
        out_specs=pl.BlockSpec((tm, tn), lambda n, m: (m, n)),
        scratch_shapes=[pltpu.VMEM((D_FF, tn), BF16)],
        compiler_params=_cparams(("arbitrary", "arbitrary"), VMEM_BUDGET),
        name="ffn_down",
    )(h, w2, x, mods)


N_RUNS = N_EXPERTS + 1


def _run_blocks_pipelined(nb, a_copy, o_copy, compute):
    def body(k, carry):
        slot = k % 2

        @pl.when(k + 1 < nb)
        def _():
            a_copy(k + 1, 1 - slot).start()

        a_copy(k, slot).wait()

        @pl.when(k >= 2)
        def _():
            o_copy(k - 2, slot).wait()

        compute(slot)
        o_copy(k, slot).start()
        return carry

    lax.fori_loop(0, nb, body, 0)

    @pl.when(nb >= 2)
    def _():
        o_copy(nb - 2, nb % 2).wait()

    @pl.when(nb >= 1)
    def _():
        o_copy(nb - 1, (nb - 1) % 2).wait()


def _run_units_pipelined(nb, a_copy, o_copy, compute, after_first_start):
    nu = (nb + 1) // 2

    def by_size(k, fn):
        full = 2 * k + 1 < nb

        @pl.when(full)
        def _():
            fn(True)

        @pl.when(jnp.logical_not(full))
        def _():
            fn(False)

    @pl.when(nu >= 1)
    def _():
        by_size(0, lambda full: a_copy(0, 0, full).start())

    after_first_start()

    def body(k, carry):
        slot = k % 2

        @pl.when(k + 1 < nu)
        def _():
            by_size(k + 1, lambda full: a_copy(k + 1, 1 - slot, full).start())

        by_size(k, lambda full: a_copy(k, slot, full).wait())

        @pl.when(k >= 2)
        def _():
            o_copy(k - 2, slot, True).wait()

        def run(full):
            compute(slot, full)
            o_copy(k, slot, full).start()

        by_size(k, run)
        return carry

    lax.fori_loop(0, nu, body, 0)

    @pl.when(nu >= 2)
    def _():
        o_copy(nu - 2, nu % 2, True).wait()

    @pl.when(nu >= 1)
    def _():
        by_size(nu - 1, lambda full: o_copy(nu - 1, (nu - 1) % 2, full).wait())


def _zero_tail_blocks(nb, obuf, o_copy):
    obuf[0] = jnp.zeros(obuf.shape[1:], obuf.dtype)

    def body(k, carry):
        o_copy(k, 0).start()
        o_copy(k, 0).wait()
        return carry

    lax.fori_loop(0, nb, body, 0)


def _weights_for_step(n, r, n_tiles, w_copies):
    step = n * N_EXPERTS + r
    slot = step % 2

    @pl.when(step == 0)
    def _():
        for c in w_copies(0, 0, 0):
            c.start()

    nxt = step + 1

    @pl.when(nxt < n_tiles * N_EXPERTS)
    def _():
        for c in w_copies(nxt // N_EXPERTS, nxt % N_EXPERTS, 1 - slot):
            c.start(priority=1)

    for c in w_copies(n, r, slot):
        c.wait()
    return slot


def _moe_up_kernel(bs_ref, nb_ref, a_hbm, w1_hbm, w3_hbm, h_hbm, wbuf, s1, s3, abuf, obuf, sem_w, sem_a, sem_o, *, j):
    n = pl.program_id(0)
    r = pl.program_id(1)
    b0 = bs_ref[r]
    nb = nb_ref[r]
    tm, tn = TM_FFN, obuf.shape[2]
    col = pl.multiple_of(n * tn, tn)

    def w_copies(nn, rr, slot):
        c = pl.multiple_of(nn * tn, tn)
        return (pltpu.make_async_copy(w1_hbm.at[j, rr, :, pl.ds(c, tn)], wbuf.at[slot, 0], sem_w.at[slot]),
                pltpu.make_async_copy(w3_hbm.at[j, rr, :, pl.ds(c, tn)], wbuf.at[slot, 1], sem_w.at[slot]))

    def unit_rows(full):
        return 2 * tm if full else tm

    def a_copy(k, slot, full):
        row = pl.multiple_of((b0 + 2 * k) * tm, tm)
        rows = unit_rows(full)
        return pltpu.make_async_copy(a_hbm.at[pl.ds(row, rows)], abuf.at[slot, pl.ds(0, rows)], sem_a.at[slot])

    def o_copy(k, slot, full):
        row = pl.multiple_of((b0 + 2 * k) * tm, tm)
        rows = unit_rows(full)
        return pltpu.make_async_copy(obuf.at[slot, pl.ds(0, rows)],
                                     h_hbm.at[pl.ds(row, rows), pl.ds(col, tn)], sem_o.at[slot])

    def compute(slot, full):
        rows = unit_rows(full)
        a = abuf[slot, pl.ds(0, rows), :]
        h1 = jnp.dot(a, s1[...], preferred_element_type=F32)
        h3 = jnp.dot(a, s3[...], preferred_element_type=F32)
        obuf[slot, pl.ds(0, rows), :] = (_silu(h1) * h3).astype(BF16)

    @pl.when(r < N_EXPERTS)
    def _():
        def weights_ready():
            slot = _weights_for_step(n, r, pl.num_programs(0), w_copies)

            @pl.when(nb > 0)
            def _():
                _cast_weight(wbuf.at[slot, 0], s1, D_MODEL)
                _cast_weight(wbuf.at[slot, 1], s3, D_MODEL)

        _run_units_pipelined(nb, a_copy, o_copy, compute, weights_ready)

    @pl.when(r == N_EXPERTS)
    def _():
        def tail_copy(k, slot):
            row = pl.multiple_of((b0 + k) * tm, tm)
            return pltpu.make_async_copy(obuf.at[slot, pl.ds(0, tm)],
                                         h_hbm.at[pl.ds(row, tm), pl.ds(col, tn)], sem_o.at[slot])

        _zero_tail_blocks(nb, obuf, tail_copy)


def _moe_up(xs, w1, w3, j, run_start, run_blocks):
    tm, tn = TM_FFN, TN_FFN1
    any_spec = pl.BlockSpec(memory_space=pl.ANY)
    grid_spec = pltpu.PrefetchScalarGridSpec(
        num_scalar_prefetch=2,
        grid=(D_FF // tn, N_RUNS),
        in_specs=[any_spec, any_spec, any_spec],
        out_specs=any_spec,
        scratch_shapes=[
            pltpu.VMEM((2, 2, D_MODEL, tn), F32),
            pltpu.VMEM((D_MODEL, tn), BF16), pltpu.VMEM((D_MODEL, tn), BF16),
            pltpu.VMEM((2, 2 * tm, D_MODEL), BF16), pltpu.VMEM((2, 2 * tm, tn), BF16),
            pltpu.SemaphoreType.DMA((2,)), pltpu.SemaphoreType.DMA((2,)), pltpu.SemaphoreType.DMA((2,)),
        ],
    )
    return pl.pallas_call(
        functools.partial(_moe_up_kernel, j=j),
        out_shape=jax.ShapeDtypeStruct((N_SLOTS, D_FF), BF16),
        grid_spec=grid_spec,
        compiler_params=_cparams(("arbitrary", "arbitrary"), VMEM_BUDGET),
        name="moe_up",
    )(run_start, run_blocks, xs, w1, w3)


def _moe_down_kernel(bs_ref, nb_ref, h_hbm, w_hbm, y_hbm, wbuf, wscr, abuf, obuf, sem_w, sem_a, sem_o, *, j):
    n = pl.program_id(0)
    r = pl.program_id(1)
    b0 = bs_ref[r]
    nb = nb_ref[r]
    tm, tn = obuf.shape[1], obuf.shape[3]
    col = pl.multiple_of(n * tn, tn)

    def w_copies(nn, rr, slot):
        c = pl.multiple_of(nn * tn, tn)
        return (pltpu.make_async_copy(w_hbm.at[j, rr, :, pl.ds(c, tn)], wbuf.at[slot], sem_w.at[slot]),)

    def a_copy(k, slot):
        row = pl.multiple_of((b0 + k) * tm, tm)
        return pltpu.make_async_copy(h_hbm.at[pl.ds(row, tm)], abuf.at[slot], sem_a.at[slot])

    def o_copy(k, slot):
        row = pl.multiple_of((b0 + k) * tm, tm)
        return pltpu.make_async_copy(obuf.at[slot], y_hbm.at[pl.ds(row, tm), :, pl.ds(col, tn)], sem_o.at[slot])

    def compute(slot):
        y = jnp.dot(abuf[slot], wscr[...], preferred_element_type=F32)
        obuf[slot] = y.reshape(obuf.shape[1:])

    @pl.when(r < N_EXPERTS)
    def _():
        @pl.when(nb > 0)
        def _():
            a_copy(0, 0).start()

        slot = _weights_for_step(n, r, pl.num_programs(0), w_copies)

        @pl.when(nb > 0)
        def _():
            _cast_weight(wbuf.at[slot], wscr, D_FF)
            _run_blocks_pipelined(nb, a_copy, o_copy, compute)

    @pl.when(r == N_EXPERTS)
    def _():
        _zero_tail_blocks(nb, obuf, o_copy)


def _moe_down(h, w2, j, run_start, run_blocks):
    tm, tn = TM_FFN, TN_FFN2
    any_spec = pl.BlockSpec(memory_space=pl.ANY)
    grid_spec = pltpu.PrefetchScalarGridSpec(
        num_scalar_prefetch=2,
        grid=(D_MODEL // tn, N_RUNS),
        in_specs=[any_spec, any_spec],
        out_specs=any_spec,
        scratch_shapes=[
            pltpu.VMEM((2, D_FF, tn), F32),
            pltpu.VMEM((D_FF, tn), BF16),
            pltpu.VMEM((2, tm, D_FF), BF16), pltpu.VMEM((2, tm, 1, tn), F32),
            pltpu.SemaphoreType.DMA((2,)), pltpu.SemaphoreType.DMA((2,)), pltpu.SemaphoreType.DMA((2,)),
        ],
    )
    return pl.pallas_call(
        functools.partial(_moe_down_kernel, j=j),
        out_shape=jax.ShapeDtypeStruct((N_SLOTS, 1, D_MODEL), F32),
        grid_spec=grid_spec,
        compiler_params=_cparams(("arbitrary", "arbitrary"), VMEM_BUDGET),
        name="moe_down",
    )(run_start, run_blocks, h, w2)


def _row_copy(src_hbm, src_row, dst, dst_row, sem):
    return pltpu.make_async_copy(src_hbm.at[pl.ds(src_row, 1)], dst.at[pl.ds(dst_row, 1)], sem)


ROW_DMA_UNROLL = 8


def _for_each_row(n_rows, fn):
    def body(c, carry):
        for u in range(ROW_DMA_UNROLL):
            fn(c * ROW_DMA_UNROLL + u, u)
        return carry

    lax.fori_loop(0, n_rows // ROW_DMA_UNROLL, body, 0)


def _gather_kernel(tok_ref, u_hbm, o_ref, buf, flat, sem):
    i = pl.program_id(0)
    nb = pl.num_programs(0)
    tm = o_ref.shape[0]

    def issue(block, slot):
        _for_each_row(tm, lambda r, u: _row_copy(
            u_hbm, tok_ref[block * tm + r], buf.at[slot], r, sem.at[slot]).start(priority=u % 2))

    @pl.when(i == 0)
    def _():
        issue(0, 0)

    @pl.when(i + 1 < nb)
    def _():
        issue(i + 1, (i + 1) % 2)

    slot = i % 2
    _for_each_row(tm, lambda r, u: _row_copy(u_hbm, 0, buf.at[slot], r, sem.at[slot]).wait())
    flat[...] = buf[slot].reshape(flat.shape)
    o_ref[...] = flat[...].astype(BF16)


def _gather_rows(u, slot_token):
    tm = TM_FFN
    grid_spec = pltpu.PrefetchScalarGridSpec(
        num_scalar_prefetch=1,
        grid=(N_SLOTS // tm,),
        in_specs=[pl.BlockSpec(memory_space=pl.ANY)],
        out_specs=pl.BlockSpec((tm, D_MODEL), lambda i, tok: (i, 0)),
        scratch_shapes=[pltpu.VMEM((2, tm, 1, D_MODEL), F32), pltpu.VMEM((tm, D_MODEL), F32),
                        pltpu.SemaphoreType.DMA((2,))],
    )
    return pl.pallas_call(
        _gather_kernel,
        out_shape=jax.ShapeDtypeStruct((N_SLOTS, D_MODEL), BF16),
        grid_spec=grid_spec,
        compiler_params=_cparams(("arbitrary",), 32 * 2**20),
        name="moe_gather",
    )(slot_token, u)


def _combine_kernel(d_ref, y_hbm, x_ref, r_ref, mod_ref, *rest, row):
    o_refs, (buf, flat, sem) = rest[:-3], rest[-3:]
    i = pl.program_id(0)
    nb = pl.num_programs(0)
    tm = x_ref.shape[0]

    def issue(block, slot):
        def start_pair(r, u):
            t = block * tm + r
            _row_copy(y_hbm, d_ref[2 * t], buf.at[slot, 0], r, sem.at[slot]).start(priority=0)
            _row_copy(y_hbm, d_ref[2 * t + 1], buf.at[slot, 1], r, sem.at[slot]).start(priority=1)

        _for_each_row(tm, start_pair)

    @pl.when(i == 0)
    def _():
        issue(0, 0)

    @pl.when(i + 1 < nb)
    def _():
        issue(i + 1, (i + 1) % 2)

    slot = i % 2

    def wait_pair(r, u):
        _row_copy(y_hbm, 0, buf.at[slot, 0], r, sem.at[slot]).wait()
        _row_copy(y_hbm, 0, buf.at[slot, 1], r, sem.at[slot]).wait()

    _for_each_row(tm, wait_pair)
    for k in range(TOP_K):
        flat[k] = buf[slot, k].reshape(flat.shape[1:])
    g1 = r_ref[:, 2:3]
    g2 = r_ref[:, 3:4]
    y = flat[0] * g1 + flat[1] * g2
    res = x_ref[...] + mod_ref[0, row:row + 1, :] * y
    if len(o_refs) == 1:
        o_refs[0][...] = res
    else:
        n_ctx = M_CTX // tm

        @pl.when(i < n_ctx)
        def _():
            o_refs[0][...] = res

        @pl.when(i >= n_ctx)
        def _():
            o_refs[1][...] = res


def _moe_combine(x, y_slots, dest, route, mods, row, split_groups=False):
    tm = TM_COMB
    if split_groups:
        n_ctx = M_CTX // tm
        out_shape = (jax.ShapeDtypeStruct((M_CTX, D_MODEL), F32), jax.ShapeDtypeStruct((M_LAT, D_MODEL), F32))
        out_specs = (pl.BlockSpec((tm, D_MODEL), lambda i, d: (jnp.minimum(i, n_ctx - 1), 0)),
                     pl.BlockSpec((tm, D_MODEL), lambda i, d: (jnp.maximum(i - n_ctx, 0), 0)))
    else:
        out_shape = jax.ShapeDtypeStruct((M_TOK, D_MODEL), F32)
        out_specs = pl.BlockSpec((tm, D_MODEL), lambda i, d: (i, 0))
    grid_spec = pltpu.PrefetchScalarGridSpec(
        num_scalar_prefetch=1,
        grid=(M_TOK // tm,),
        in_specs=[
            pl.BlockSpec(memory_space=pl.ANY),
            pl.BlockSpec((tm, D_MODEL), lambda i, d: (i, 0)),
            pl.BlockSpec((tm, LANES), lambda i, d: (i, 0)),
            pl.BlockSpec((1, MOD_ROWS, D_MODEL), lambda i, d: (_group_of_tile(i, tm), 0, 0)),
        ],
        out_specs=out_specs,
        scratch_shapes=[pltpu.VMEM((2, TOP_K, tm, 1, D_MODEL), F32), pltpu.VMEM((TOP_K, tm, D_MODEL), F32),
                        pltpu.SemaphoreType.DMA((2,))],
    )
    return pl.pallas_call(
        functools.partial(_combine_kernel, row=row),
        out_shape=out_shape,
        grid_spec=grid_spec,
        compiler_params=_cparams(("arbitrary",), 44 * 2**20),
        name="moe_combine",
    )(dest.reshape(-1), y_slots, x, route, mods)


def _moe_plan(route):
    tm = TM_FFN
    expert = route[:, :TOP_K].astype(jnp.int32).reshape(-1)
    onehot = (expert[:, None] == jnp.arange(N_EXPERTS, dtype=jnp.int32)[None, :]).astype(jnp.int32)
    csum = jnp.cumsum(onehot, axis=0)
    rank = jnp.take_along_axis(csum, expert[:, None], axis=1)[:, 0] - 1
    counts = csum[-1]
    padded = (counts + tm - 1) // tm * tm
    pend = jnp.cumsum(padded)
    pstart = pend - padded
    dest = pstart[expert] + rank
    token = jnp.arange(M_TOK * TOP_K, dtype=jnp.int32) // TOP_K
    slot_token = jnp.zeros((N_SLOTS,), jnp.int32).at[dest].set(token)
    n_valid = pend[-1:] // tm
    run_start = jnp.concatenate([pstart // tm, n_valid]).astype(jnp.int32)
    run_blocks = jnp.concatenate([padded // tm, N_SLOTS_BLOCKS - n_valid]).astype(jnp.int32)
    return slot_token, dest.astype(jnp.int32).reshape(M_TOK, TOP_K), run_start, run_blocks


def kernel(x_prompt, x_sample, cache_k, cache_v, c, c_ctx, ada_w, ada_b, norm1_g, norm2_g, w_in, qn_g, kn_g, pool_w, pool_scale, rel_bias, w_bp, w_ba, w_out, ffn_w1, ffn_w3, ffn_w2, router_w, router_b, moe_w1, moe_w3, moe_w2):
    x = (x_prompt.reshape(M_CTX, D_MODEL), x_sample.reshape(M_LAT, D_MODEL))
    cond8 = jnp.zeros((MOD_ROWS, D_MODEL), F32).at[0].set(c_ctx).at[1:1 + DEC_BATCH].set(c)
    mods_all = _adaln_all(cond8, ada_w, ada_b)
    mods_all = mods_all[:, :N_GROUPS].reshape(DEPTH, N_GROUPS, 6, D_MODEL)
    mods_all = jnp.pad(mods_all, ((0, 0), (0, 0), (0, MOD_ROWS - 6), (0, 0)))
    t2_all = _bias_tables(rel_bias)

    new_k = jnp.zeros((BATCH, DEPTH, N_HEADS, SEQ, HEAD_DIM), F32)
    new_v = jnp.zeros((BATCH, DEPTH, N_HEADS, SEQ, HEAD_DIM), F32)
    for l in range(DEPTH):
        j = l // 2
        mods = mods_all[l]
        u = _norm_mod(x, norm1_g[l], mods, 0)
        proj = _in_proj(u, w_in, l)
        pool_out = _pool_mixer(proj, pool_w[l], pool_scale[l])
        attn_ctx, new_k, new_v = _ctx_attention(proj, qn_g[l], kn_g[l], l, new_k, new_v)
        attn_lat = _lat_attention(proj, cache_k, cache_v, t2_all, qn_g[l], kn_g[l], l)
        merged = _mix_merge(pool_out, attn_ctx, attn_lat, w_bp, w_ba, proj, l)
        x = _out_proj_residual(merged, w_out, l, x, mods, 2)
        if l % 2 == 0:
            u2 = _norm_mod(x, norm2_g[l], mods, 3)
            h = _ffn_up(u2, ffn_w1, ffn_w3, j)
            x = _ffn_down(h, ffn_w2, j, x, mods, 5)
        else:
            u2, route = _norm_route(x, norm2_g[l], mods, 3, router_w[j], router_b[j])
            slot_token, dest, run_start, run_blocks = _moe_plan(route)
            xs = _gather_rows(u2, slot_token)
            h = _moe_up(xs, moe_w1, moe_w3, j, run_start, run_blocks)
            y_slots = _moe_down(h, moe_w2, j, run_start, run_blocks)
            x = _moe_combine(x, y_slots, dest, route, mods, 5, split_groups=(l == DEPTH - 1))

    y_ctx, y_lat = x
    return (y_ctx.reshape(BATCH, SEQ, D_MODEL), y_lat.reshape(DEC_BATCH, DEC_SEQ, D_MODEL), new_k, new_v)
```

```python
import functools

import jax
import jax.numpy as jnp
from jax import lax
from jax.experimental import pallas as pl
from jax.experimental.pallas import tpu as pltpu

F32 = jnp.float32
BF16 = jnp.bfloat16

D_MODEL = 2048
BATCH = 32
SEQ = 256
DEPTH = 4
DEC_BATCH = 2
DEC_SEQ = 2048
PAST_LEN = 512
GRID_W = 64
GRID_ROWS = DEC_SEQ // GRID_W
POOL_WIDTH = D_MODEL // 2
N_POOL_GROUPS = 4
POOL_GROUP_W = POOL_WIDTH // N_POOL_GROUPS
POOL_WINDOWS = (2, 4, 8, 16)
ATTN_WIDTH = D_MODEL // 2
HEAD_DIM = 64
N_HEADS = ATTN_WIDTH // HEAD_DIM
WIN_ROWS = 8
WIN_COLS = 16
D_FF = ((8 * D_MODEL // 3 + 255) // 256) * 256
N_EXPERTS = 8
TOP_K = 2
PROJ_WIDTH = POOL_WIDTH + 3 * ATTN_WIDTH + 2 * D_MODEL
RMS_EPS = 1e-6
NEG_INF = -1e30
ATTN_SCALE = HEAD_DIM ** -0.5

M_CTX = BATCH * SEQ
M_LAT = DEC_BATCH * DEC_SEQ
M_TOK = M_CTX + M_LAT
N_GROUPS = 1 + DEC_BATCH
MOD_ROWS = 8

LANES = 128
V7X_VMEM_BYTES = 64 * 1024 * 1024
VMEM_BUDGET = V7X_VMEM_BYTES - 8 * 1024 * 1024

CAST_ROWS = 256

TM_NORM = 512
TM_IN, TN_IN = 1024, 1024
TM_MIX, TN_MIX = 512, 1024
TM_OUT, TN_OUT = 1024, 1024
TM_OUT_PAIR = 512
TM_FFN, TN_FFN1, TN_FFN2 = 512, 512, 512
TM_FFN_UP_DENSE = 1024
TM_COMB = 256
POOL_TILE = 256
POOL_HALO = 8
LAT_QROWS = 8
LAT_QBLK = LAT_QROWS * GRID_W
LAT_KROWS = 16
LAT_KWIN = LAT_KROWS * GRID_W
N_SLOTS_BLOCKS = (M_TOK * TOP_K) // TM_FFN + N_EXPERTS
N_SLOTS = N_SLOTS_BLOCKS * TM_FFN


def _cparams(semantics, vmem_bytes):
    return pltpu.CompilerParams(dimension_semantics=semantics, vmem_limit_bytes=int(vmem_bytes))


def _group_of_tile(m, tm):
    start = m * tm
    return jnp.where(start < M_CTX, 0, 1 + (start - M_CTX) // DEC_SEQ)


def _sigmoid(z):
    return 1.0 / (1.0 + jnp.exp(-z))


def _silu(z):
    return z / (1.0 + jnp.exp(-z))


def _rms(x):
    return x * lax.rsqrt(jnp.mean(x * x, axis=-1, keepdims=True) + RMS_EPS)


def _cast_weight(w_ref, scr_ref, k_rows):
    def body(i, carry):
        r = pl.multiple_of(i * CAST_ROWS, CAST_ROWS)
        scr_ref[pl.ds(r, CAST_ROWS), :] = w_ref[pl.ds(r, CAST_ROWS), :].astype(BF16)
        return carry

    lax.fori_loop(0, k_rows // CAST_ROWS, body, 0)


def _ada_kernel(c_ref, w_ref, b_ref, o_ref):
    s = _silu(c_ref[...]).astype(BF16)
    w = w_ref[0].astype(BF16)
    o_ref[0] = jnp.dot(s, w, preferred_element_type=F32) + b_ref[0]


def _adaln_all(cond8, ada_w, ada_b):
    tn = 1024
    n_out = 6 * D_MODEL
    return pl.pallas_call(
        _ada_kernel,
        out_shape=jax.ShapeDtypeStruct((DEPTH, MOD_ROWS, n_out), F32),
        grid=(DEPTH, n_out // tn),
        in_specs=[
            pl.BlockSpec((MOD_ROWS, D_MODEL), lambda l, n: (0, 0)),
            pl.BlockSpec((1, D_MODEL, tn), lambda l, n: (l, 0, n)),
            pl.BlockSpec((1, 1, tn), lambda l, n: (l, 0, n)),
        ],
        out_specs=pl.BlockSpec((1, MOD_ROWS, tn), lambda l, n: (l, 0, n)),
        compiler_params=_cparams(("arbitrary", "arbitrary"), 40 * 2**20),
        name="adaln",
    )(cond8, ada_w, ada_b.reshape(DEPTH, 1, n_out))


def _token_specs(x, tm, width, col_of):
    if not isinstance(x, tuple):
        return [pl.BlockSpec((tm, width), lambda *g: col_of(*g))], [x]
    n_ctx = M_CTX // tm

    def ctx_map(*g):
        m, c = col_of(*g)
        return (jnp.minimum(m, n_ctx - 1), c)

    def lat_map(*g):
        m, c = col_of(*g)
        return (jnp.maximum(m - n_ctx, 0), c)

    return [pl.BlockSpec((tm, width), ctx_map), pl.BlockSpec((tm, width), lat_map)], list(x)


def _pick_tile(m, tm, refs):
    if len(refs) == 1:
        return refs[0][...]
    return jnp.where(m < M_CTX // tm, refs[0][...], refs[1][...])


def _norm_mod_kernel_x(*refs, row):
    x_refs, (g_ref, mod_ref, o_ref) = refs[:-3], refs[-3:]
    y = _rms(_pick_tile(pl.program_id(0), TM_NORM, x_refs)) * g_ref[...]
    sh = mod_ref[0, row:row + 1, :]
    sc = mod_ref[0, row + 1:row + 2, :]
    o_ref[...] = (y * (1.0 + sc) + sh).astype(o_ref.dtype)


def _norm_mod(x, g, mods, row):
    tm = TM_NORM
    x_specs, x_args = _token_specs(x, tm, D_MODEL, lambda m: (m, 0))
    return pl.pallas_call(
        functools.partial(_norm_mod_kernel_x, row=row),
        out_shape=jax.ShapeDtypeStruct((M_TOK, D_MODEL), BF16),
        grid=(M_TOK // tm,),
        in_specs=x_specs + [
            pl.BlockSpec((1, D_MODEL), lambda m: (0, 0)),
            pl.BlockSpec((1, MOD_ROWS, D_MODEL), lambda m: (_group_of_tile(m, tm), 0, 0)),
        ],
        out_specs=pl.BlockSpec((tm, D_MODEL), lambda m: (m, 0)),
        compiler_params=_cparams(("parallel",), 44 * 2**20),
        name="norm_mod",
    )(*x_args, g.reshape(1, D_MODEL), mods)


def _norm_route_kernel(x_ref, g_ref, mod_ref, rwh_ref, rwl_ref, rb_ref, u_ref, r_ref, *, row):
    y = _rms(x_ref[...]) * g_ref[...]
    sh = mod_ref[0, row:row + 1, :]
    sc = mod_ref[0, row + 1:row + 2, :]
    u = y * (1.0 + sc) + sh
    u_ref[...] = u.reshape(u_ref.shape)
    u_hi = u.astype(BF16)
    u_lo = (u - u_hi.astype(F32)).astype(BF16)
    logits = (jnp.dot(u_hi, rwh_ref[...], preferred_element_type=F32)
              + jnp.dot(u_hi, rwl_ref[...], preferred_element_type=F32)
              + jnp.dot(u_lo, rwh_ref[...], preferred_element_type=F32)) + rb_ref[...]
    lane = lax.broadcasted_iota(jnp.int32, logits.shape, 1).astype(F32)
    big = float(LANES)
    m1 = jnp.max(logits, axis=-1, keepdims=True)
    i1 = jnp.min(jnp.where(logits == m1, lane, big), axis=-1, keepdims=True)
    rest = jnp.where(lane == i1, -jnp.inf, logits)
    m2 = jnp.max(rest, axis=-1, keepdims=True)
    i2 = jnp.min(jnp.where(rest == m2, lane, big), axis=-1, keepdims=True)
    e2 = jnp.exp(m2 - m1)
    den = 1.0 + e2
    r_ref[...] = jnp.where(lane == 0.0, i1,
                           jnp.where(lane == 1.0, i2,
                                     jnp.where(lane == 2.0, 1.0 / den,
                                               jnp.where(lane == 3.0, e2 / den, 0.0))))


def _norm_route(x, g, mods, row, router_w, router_b):
    tm = TM_NORM
    rw = jnp.zeros((D_MODEL, LANES), F32).at[:, :N_EXPERTS].set(router_w)
    rw_hi = rw.astype(BF16)
    rw_lo = (rw - rw_hi.astype(F32)).astype(BF16)
    rb = jnp.full((1, LANES), -jnp.inf, F32).at[0, :N_EXPERTS].set(router_b)
    return pl.pallas_call(
        functools.partial(_norm_route_kernel, row=row),
        out_shape=(jax.ShapeDtypeStruct((M_TOK, 1, D_MODEL), F32),
                   jax.ShapeDtypeStruct((M_TOK, LANES), F32)),
        grid=(M_TOK // tm,),
        in_specs=[
            pl.BlockSpec((tm, D_MODEL), lambda m: (m, 0)),
            pl.BlockSpec((1, D_MODEL), lambda m: (0, 0)),
            pl.BlockSpec((1, MOD_ROWS, D_MODEL), lambda m: (_group_of_tile(m, tm), 0, 0)),
            pl.BlockSpec((D_MODEL, LANES), lambda m: (0, 0)),
            pl.BlockSpec((D_MODEL, LANES), lambda m: (0, 0)),
            pl.BlockSpec((1, LANES), lambda m: (0, 0)),
        ],
        out_specs=(pl.BlockSpec((tm, 1, D_MODEL), lambda m: (m, 0, 0)),
                   pl.BlockSpec((tm, LANES), lambda m: (m, 0))),
        compiler_params=_cparams(("parallel",), 40 * 2**20),
        name="norm_route",
    )(x, g.reshape(1, D_MODEL), mods, rw_hi, rw_lo, rb)


def _in_proj_kernel(a_ref, w_ref, o_ref, wscr):
    @pl.when(pl.program_id(1) == 0)
    def _():
        _cast_weight(w_ref.at[0], wscr, D_MODEL)

    o_ref[...] = jnp.dot(a_ref[...], wscr[...], preferred_element_type=F32)


def _in_proj(u, w_in, layer):
    tm, tn = TM_IN, TN_IN
    return pl.pallas_call(
        _in_proj_kernel,
        out_shape=jax.ShapeDtypeStruct((M_TOK, PROJ_WIDTH), F32),
        grid=(PROJ_WIDTH // tn, M_TOK // tm),
        in_specs=[
            pl.BlockSpec((tm, D_MODEL), lambda n, m: (m, 0)),
            pl.BlockSpec((1, D_MODEL, tn), lambda n, m: (layer, 0, n)),
        ],
        out_specs=pl.BlockSpec((tm, tn), lambda n, m: (m, n)),
        scratch_shapes=[pltpu.VMEM((D_MODEL, tn), BF16)],
        compiler_params=_cparams(("arbitrary", "arbitrary"), VMEM_BUDGET),
        name="in_proj",
    )(u, w_in)


def _pool_kernel(xm_ref, xp_ref, xn_ref, pw_ref, ps_ref, o_ref, pad):
    i = pl.program_id(0)
    n_ctx_tiles = M_CTX // POOL_TILE
    tiles_per_lat = DEC_SEQ // POOL_TILE
    is_lat = i >= n_ctx_tiles
    j = (i - n_ctx_tiles) % tiles_per_lat
    has_prev = jnp.logical_and(is_lat, j != 0)
    has_next = jnp.logical_and(is_lat, j != tiles_per_lat - 1)
    pad[0:POOL_HALO, :] = jnp.where(has_prev, xp_ref[...], 0.0)
    pad[POOL_HALO:POOL_HALO + POOL_TILE, :] = xm_ref[...]
    pad[POOL_HALO + POOL_TILE:, :] = jnp.where(has_next, xn_ref[...], 0.0)
    off = jnp.where(is_lat, j * POOL_TILE, 0)
    seq_len = jnp.where(is_lat, DEC_SEQ, SEQ)
    pos = lax.broadcasted_iota(jnp.int32, (POOL_TILE, 1), 0) + off
    for g, win in enumerate(POOL_WINDOWS):
        lanes = slice(g * POOL_GROUP_W, (g + 1) * POOL_GROUP_W)
        back, fwd = win // 2, win - win // 2
        acc = pad[pl.ds(POOL_HALO - back, POOL_TILE), lanes]
        for d in range(-back + 1, fwd):
            acc = acc + pad[pl.ds(POOL_HALO + d, POOL_TILE), lanes]
        cnt = (jnp.minimum(pos + fwd, seq_len) - jnp.maximum(pos - back, 0)).astype(F32)
        pooled = (acc / cnt - xm_ref[:, lanes]).astype(BF16)
        mixed = jnp.dot(pooled, pw_ref[g].astype(BF16), preferred_element_type=F32)
        o_ref[:, lanes] = (mixed * ps_ref[:, lanes]).astype(BF16)


def _pool_mixer(proj, pool_w, pool_scale):
    t, h = POOL_TILE, POOL_HALO
    hb = t // h
    last_halo = M_TOK // h - 1
    return pl.pallas_call(
        _pool_kernel,
        out_shape=jax.ShapeDtypeStruct((M_TOK, POOL_WIDTH), BF16),
        grid=(M_TOK // t,),
        in_specs=[
            pl.BlockSpec((t, POOL_WIDTH), lambda i: (i, 0)),
            pl.BlockSpec((h, POOL_WIDTH), lambda i: (jnp.maximum(i * hb - 1, 0), 0)),
            pl.BlockSpec((h, POOL_WIDTH), lambda i: (jnp.minimum((i + 1) * hb, last_halo), 0)),
            pl.BlockSpec((N_POOL_GROUPS, POOL_GROUP_W, POOL_GROUP_W), lambda i: (0, 0, 0)),
            pl.BlockSpec((1, POOL_WIDTH), lambda i: (0, 0)),
        ],
        out_specs=pl.BlockSpec((t, POOL_WIDTH), lambda i: (i, 0)),
        scratch_shapes=[pltpu.VMEM((t + 2 * h, POOL_WIDTH), F32)],
        compiler_params=_cparams(("parallel",), 24 * 2**20),
        name="pool_mixer",
    )(proj, proj, proj, pool_w, pool_scale.reshape(1, POOL_WIDTH))


PAIR_W = 2 * HEAD_DIM


def _pair_group_ones():
    r = lax.broadcasted_iota(jnp.int32, (PAIR_W, PAIR_W), 0) // HEAD_DIM
    c = lax.broadcasted_iota(jnp.int32, (PAIR_W, PAIR_W), 1) // HEAD_DIM
    return jnp.where(r == c, 1.0, 0.0).astype(BF16)


def _head_masks():
    lane = lax.broadcasted_iota(jnp.int32, (1, PAIR_W), 1)
    return (lane < HEAD_DIM, lane >= HEAD_DIM)


def _rms_pair(x, ones):
    y2 = x * x
    hi = y2.astype(BF16)
    lo = (y2 - hi.astype(F32)).astype(BF16)
    ss = jnp.dot(hi, ones, preferred_element_type=F32) + jnp.dot(lo, ones, preferred_element_type=F32)
    return x * lax.rsqrt(ss * (1.0 / HEAD_DIM) + RMS_EPS)


def _softmax_rows(s):
    e = jnp.exp(s - jnp.max(s, axis=-1, keepdims=True))
    return e * (1.0 / jnp.sum(e, axis=-1, keepdims=True))


_NT_DIMS = (((1,), (1,)), ((), ()))


def _ctx_attn_kernel(q_ref, k_ref, v_ref, qg_ref, kg_ref, _ck_in, _cv_in, o_ref, ko_ref, vo_ref):
    qg = qg_ref[...] * ATTN_SCALE
    kg = kg_ref[...]
    ones = _pair_group_ones()
    masks = _head_masks()
    for p in range(N_HEADS // 2):
        sl = slice(p * PAIR_W, (p + 1) * PAIR_W)
        kn = _rms_pair(k_ref[:, sl], ones) * kg
        v = v_ref[:, sl]
        for hh in range(2):
            lanes = slice(hh * HEAD_DIM, (hh + 1) * HEAD_DIM)
            ko_ref[0, 0, 2 * p + hh] = kn[:, lanes]
            vo_ref[0, 0, 2 * p + hh] = v[:, lanes]
        qn = _rms_pair(q_ref[:, sl], ones) * qg
        kb = kn.astype(BF16)
        o = jnp.zeros((SEQ, PAIR_W), F32)
        for hh in range(2):
            qh = jnp.where(masks[hh], qn, 0.0).astype(BF16)
            s = lax.dot_general(qh, kb, _NT_DIMS, preferred_element_type=F32)
            pr = _softmax_rows(s).astype(BF16)
            vh = jnp.where(masks[hh], v, 0.0).astype(BF16)
            o = o + jnp.dot(pr, vh, preferred_element_type=F32)
        o_ref[:, sl] = o.astype(BF16)


def _pair_gain(g):
    return jnp.concatenate([g, g]).reshape(1, PAIR_W)


def _ctx_attention(proj, qn_g, kn_g, layer, new_k, new_v):
    q_blk = POOL_WIDTH // ATTN_WIDTH
    cache_shape = jax.ShapeDtypeStruct((BATCH, DEPTH, N_HEADS, SEQ, HEAD_DIM), F32)
    cache_spec = pl.BlockSpec((1, 1, N_HEADS, SEQ, HEAD_DIM), lambda b: (b, layer, 0, 0, 0))
    return pl.pallas_call(
        _ctx_attn_kernel,
        out_shape=(jax.ShapeDtypeStruct((M_CTX, ATTN_WIDTH), BF16), cache_shape, cache_shape),
        grid=(BATCH,),
        in_specs=[
            pl.BlockSpec((SEQ, ATTN_WIDTH), lambda b: (b, q_blk)),
            pl.BlockSpec((SEQ, ATTN_WIDTH), lambda b: (b, q_blk + 1)),
            pl.BlockSpec((SEQ, ATTN_WIDTH), lambda b: (b, q_blk + 2)),
            pl.BlockSpec((1, PAIR_W), lambda b: (0, 0)),
            pl.BlockSpec((1, PAIR_W), lambda b: (0, 0)),
            pl.BlockSpec(memory_space=pl.ANY),
            pl.BlockSpec(memory_space=pl.ANY),
        ],
        out_specs=(pl.BlockSpec((SEQ, ATTN_WIDTH), lambda b: (b, 0)), cache_spec, cache_spec),
        input_output_aliases={5: 1, 6: 2},
        compiler_params=_cparams(("arbitrary",), 40 * 2**20),
        name="ctx_attention",
    )(proj, proj, proj, _pair_gain(qn_g), _pair_gain(kn_g), new_k, new_v)


def _lat_attn_kernel(q_ref, k_ref, v_ref, ck_ref, cv_ref, t2_ref, qg_ref, kg_ref, o_ref,
                     kscr, vscr, ckscr, cvscr, sscr):
    blk = pl.program_id(2)

    ones = _pair_group_ones()
    masks = _head_masks()

    @pl.when(blk == 0)
    def _():
        kscr[...] = (_rms_pair(k_ref[...], ones) * kg_ref[...]).astype(BF16)
        ckscr[...] = jnp.concatenate([ck_ref[0, 0, 0], ck_ref[0, 0, 1]], axis=-1).astype(BF16)
        v = v_ref[...]
        cv = jnp.concatenate([cv_ref[0, 0, 0], cv_ref[0, 0, 1]], axis=-1)
        for hh in range(2):
            vscr[hh] = jnp.where(masks[hh], v, 0.0).astype(BF16)
            cvscr[hh] = jnp.where(masks[hh], cv, 0.0).astype(BF16)

    r0 = blk * LAT_QROWS
    w0 = jnp.clip(r0 - WIN_ROWS // 2, 0, GRID_ROWS - LAT_KROWS)
    kstart = pl.multiple_of(w0 * GRID_W, 256)
    lane_lo = lax.broadcasted_iota(jnp.int32, (GRID_W, 2 * GRID_W), 1) < GRID_W
    qn = _rms_pair(q_ref[...], ones) * (qg_ref[...] * ATTN_SCALE)
    kw = kscr[pl.ds(kstart, LAT_KWIN), :]
    ckb = ckscr[...]
    o = jnp.zeros((LAT_QBLK, PAIR_W), F32)
    for hh in range(2):
        qh = jnp.where(masks[hh], qn, 0.0).astype(BF16)
        s = lax.dot_general(qh, kw, _NT_DIMS, preferred_element_type=F32)
        for qi in range(LAT_QROWS):
            qr = r0 + qi
            rs = jnp.clip(qr - WIN_ROWS // 2, 0, GRID_ROWS - WIN_ROWS)
            for jp in range(LAT_KROWS // 2):
                kr0 = w0 + 2 * jp
                idx = jnp.clip(kr0 - qr + WIN_ROWS, 0, 2 * WIN_ROWS - 1)
                ok0 = jnp.logical_and(kr0 >= rs, kr0 < rs + WIN_ROWS).astype(jnp.int32)
                ok1 = jnp.logical_and(kr0 + 1 >= rs, kr0 + 1 < rs + WIN_ROWS).astype(jnp.int32)
                ok = jnp.where(lane_lo, ok0, ok1) > 0
                tile = jnp.where(ok, t2_ref[0, hh, idx], NEG_INF)
                rows = slice(qi * GRID_W, (qi + 1) * GRID_W)
                cols = slice(jp * 2 * GRID_W, (jp + 1) * 2 * GRID_W)
                sscr[rows, cols] = s[rows, cols] + tile
        s_loc = sscr[...]
        s_ctx = lax.dot_general(qh, ckb, _NT_DIMS, preferred_element_type=F32)
        m = jnp.maximum(jnp.max(s_loc, axis=-1, keepdims=True), jnp.max(s_ctx, axis=-1, keepdims=True))
        e_loc = jnp.exp(s_loc - m)
        e_ctx = jnp.exp(s_ctx - m)
        inv = 1.0 / (jnp.sum(e_loc, axis=-1, keepdims=True) + jnp.sum(e_ctx, axis=-1, keepdims=True))
        vh = vscr[hh, pl.ds(kstart, LAT_KWIN), :]
        o = o + inv * (jnp.dot(e_loc.astype(BF16), vh, preferred_element_type=F32)
                       + jnp.dot(e_ctx.astype(BF16), cvscr[hh], preferred_element_type=F32))
    o_ref[...] = o.astype(BF16)


def _bias_tables(rel_bias):
    col = jnp.arange(GRID_W)
    col_start = jnp.clip(col - WIN_COLS // 2, 0, GRID_W - WIN_COLS)
    col_mask = (col[None, :] >= col_start[:, None]) & (col[None, :] < col_start[:, None] + WIN_COLS)
    dc_idx = jnp.clip(col[None, :] - col[:, None], -(WIN_COLS - 1), WIN_COLS - 1) + (WIN_COLS - 1)
    onehot = (dc_idx[:, :, None] == jnp.arange(2 * WIN_COLS - 1)[None, None, :]).astype(F32)
    t = jnp.einsum('lhrc,qkc->lhrqk', rel_bias.astype(F32), onehot, precision=lax.Precision.HIGHEST)
    t = jnp.where(col_mask, t, NEG_INF)
    neg = jnp.full(t.shape[:2] + (1,) + t.shape[3:], NEG_INF, F32)
    tp = jnp.concatenate([neg, t, neg], axis=2)
    return jnp.concatenate([tp[:, :, :2 * WIN_ROWS], tp[:, :, 1:2 * WIN_ROWS + 1]], axis=-1)


def _lat_attention(proj, cache_k, cache_v, t2, qn_g, kn_g, layer):
    hp_w = 2 * HEAD_DIM
    q_col = POOL_WIDTH // hp_w
    k_col = (POOL_WIDTH + ATTN_WIDTH) // hp_w
    v_col = (POOL_WIDTH + 2 * ATTN_WIDTH) // hp_w
    lat_row_seq = M_CTX // DEC_SEQ
    lat_row_blk = M_CTX // LAT_QBLK
    n_blk = DEC_SEQ // LAT_QBLK
    return pl.pallas_call(
        _lat_attn_kernel,
        out_shape=jax.ShapeDtypeStruct((M_LAT, ATTN_WIDTH), BF16),
        grid=(DEC_BATCH, N_HEADS // 2, n_blk),
        in_specs=[
            pl.BlockSpec((LAT_QBLK, hp_w), lambda b, hp, blk: (lat_row_blk + b * n_blk + blk, q_col + hp)),
            pl.BlockSpec((DEC_SEQ, hp_w), lambda b, hp, blk: (lat_row_seq + b, k_col + hp)),
            pl.BlockSpec((DEC_SEQ, hp_w), lambda b, hp, blk: (lat_row_seq + b, v_col + hp)),
            pl.BlockSpec((1, 1, 2, PAST_LEN, HEAD_DIM), lambda b, hp, blk: (b, layer, hp, 0, 0)),
            pl.BlockSpec((1, 1, 2, PAST_LEN, HEAD_DIM), lambda b, hp, blk: (b, layer, hp, 0, 0)),
            pl.BlockSpec((1, 2, 2 * WIN_ROWS, GRID_W, 2 * GRID_W), lambda b, hp, blk: (layer, hp, 0, 0, 0)),
            pl.BlockSpec((1, PAIR_W), lambda b, hp, blk: (0, 0)),
            pl.BlockSpec((1, PAIR_W), lambda b, hp, blk: (0, 0)),
        ],
        out_specs=pl.BlockSpec((LAT_QBLK, hp_w), lambda b, hp, blk: (b * n_blk + blk, hp)),
        scratch_shapes=[
            pltpu.VMEM((DEC_SEQ, PAIR_W), BF16),
            pltpu.VMEM((2, DEC_SEQ, PAIR_W), BF16),
            pltpu.VMEM((PAST_LEN, PAIR_W), BF16),
            pltpu.VMEM((2, PAST_LEN, PAIR_W), BF16),
            pltpu.VMEM((LAT_QBLK, LAT_KWIN), F32),
        ],
        compiler_params=_cparams(("arbitrary", "arbitrary", "arbitrary"), 48 * 2**20),
        name="lat_attention",
    )(proj, proj, proj, cache_k, cache_v, t2, _pair_gain(qn_g), _pair_gain(kn_g))


def _mix_kernel(p_ref, ac_ref, al_ref, wbp_ref, wba_ref, gp_ref, ga_ref, o_ref, sbp, sba):
    m = pl.program_id(1)

    @pl.when(m == 0)
    def _():
        _cast_weight(wbp_ref.at[0], sbp, POOL_WIDTH)
        _cast_weight(wba_ref.at[0], sba, ATTN_WIDTH)

    a = jnp.where(m < M_CTX // TM_MIX, ac_ref[...], al_ref[...])
    yp = jnp.dot(p_ref[...], sbp[...], preferred_element_type=F32)
    ya = jnp.dot(a, sba[...], preferred_element_type=F32)
    o_ref[...] = (_sigmoid(gp_ref[...]) * yp + _sigmoid(ga_ref[...]) * ya).astype(BF16)


def _mix_merge(pool_out, attn_ctx, attn_lat, w_bp, w_ba, proj, layer):
    tm, tn = TM_MIX, TN_MIX
    gp_col = (POOL_WIDTH + 3 * ATTN_WIDTH) // tn
    ga_col = gp_col + D_MODEL // tn
    n_ctx = M_CTX // tm
    return pl.pallas_call(
        _mix_kernel,
        out_shape=jax.ShapeDtypeStruct((M_TOK, D_MODEL), BF16),
        grid=(D_MODEL // tn, M_TOK // tm),
        in_specs=[
            pl.BlockSpec((tm, POOL_WIDTH), lambda n, m: (m, 0)),
            pl.BlockSpec((tm, ATTN_WIDTH), lambda n, m: (jnp.minimum(m, n_ctx - 1), 0)),
            pl.BlockSpec((tm, ATTN_WIDTH), lambda n, m: (jnp.maximum(m - n_ctx, 0), 0)),
            pl.BlockSpec((1, POOL_WIDTH, tn), lambda n, m: (layer, 0, n)),
            pl.BlockSpec((1, ATTN_WIDTH, tn), lambda n, m: (layer, 0, n)),
            pl.BlockSpec((tm, tn), lambda n, m: (m, gp_col + n)),
            pl.BlockSpec((tm, tn), lambda n, m: (m, ga_col + n)),
        ],
        out_specs=pl.BlockSpec((tm, tn), lambda n, m: (m, n)),
        scratch_shapes=[pltpu.VMEM((POOL_WIDTH, tn), BF16), pltpu.VMEM((ATTN_WIDTH, tn), BF16)],
        compiler_params=_cparams(("arbitrary", "arbitrary"), 48 * 2**20),
        name="mix_merge",
    )(pool_out, attn_ctx, attn_lat, w_bp, w_ba, proj, proj)


def _out_proj_kernel(a_ref, w_ref, mod_ref, *rest, row):
    x_refs, (o_ref, wscr) = rest[:-2], rest[-2:]
    m = pl.program_id(1)

    @pl.when(m == 0)
    def _():
        _cast_weight(w_ref.at[0], wscr, D_MODEL)

    y = jnp.dot(a_ref[...], wscr[...], preferred_element_type=F32)
    o_ref[...] = _pick_tile(m, o_ref.shape[0], x_refs) + mod_ref[0, row:row + 1, :] * y


def _out_proj_residual(merged, w_out, layer, x, mods, row):
    tm, tn = (TM_OUT_PAIR if isinstance(x, tuple) else TM_OUT), TN_OUT
    x_specs, x_args = _token_specs(x, tm, tn, lambda n, m: (m, n))
    return pl.pallas_call(
        functools.partial(_out_proj_kernel, row=row),
        out_shape=jax.ShapeDtypeStruct((M_TOK, D_MODEL), F32),
        grid=(D_MODEL // tn, M_TOK // tm),
        in_specs=[
            pl.BlockSpec((tm, D_MODEL), lambda n, m: (m, 0)),
            pl.BlockSpec((1, D_MODEL, tn), lambda n, m: (layer, 0, n)),
            pl.BlockSpec((1, MOD_ROWS, tn), lambda n, m: (_group_of_tile(m, tm), 0, n)),
        ] + x_specs,
        out_specs=pl.BlockSpec((tm, tn), lambda n, m: (m, n)),
        scratch_shapes=[pltpu.VMEM((D_MODEL, tn), BF16)],
        compiler_params=_cparams(("arbitrary", "arbitrary"), VMEM_BUDGET),
        name="out_proj",
    )(merged, w_out, mods, *x_args)


def _ffn_up_kernel(a_ref, w1_ref, w3_ref, o_ref, s1, s3):
    @pl.when(pl.program_id(1) == 0)
    def _():
        _cast_weight(w1_ref.at[0], s1, D_MODEL)
        _cast_weight(w3_ref.at[0], s3, D_MODEL)

    a = a_ref[...]
    h1 = jnp.dot(a, s1[...], preferred_element_type=F32)
    h3 = jnp.dot(a, s3[...], preferred_element_type=F32)
    o_ref[...] = (_silu(h1) * h3).astype(BF16)


def _ffn_up(a, w1, w3, j):
    tm, tn = TM_FFN_UP_DENSE, TN_FFN1
    return pl.pallas_call(
        _ffn_up_kernel,
        out_shape=jax.ShapeDtypeStruct((M_TOK, D_FF), BF16),
        grid=(D_FF // tn, M_TOK // tm),
        in_specs=[
            pl.BlockSpec((tm, D_MODEL), lambda n, m: (m, 0)),
            pl.BlockSpec((1, D_MODEL, tn), lambda n, m: (j, 0, n)),
            pl.BlockSpec((1, D_MODEL, tn), lambda n, m: (j, 0, n)),
        ],
        out_specs=pl.BlockSpec((tm, tn), lambda n, m: (m, n)),
        scratch_shapes=[pltpu.VMEM((D_MODEL, tn), BF16), pltpu.VMEM((D_MODEL, tn), BF16)],
        compiler_params=_cparams(("arbitrary", "arbitrary"), 48 * 2**20),
        name="ffn_up",
    )(a, w1, w3)


def _ffn_down_kernel(a_ref, w_ref, x_ref, mod_ref, o_ref, wscr, *, row):
    @pl.when(pl.program_id(1) == 0)
    def _():
        _cast_weight(w_ref.at[0], wscr, D_FF)

    y = jnp.dot(a_ref[...], wscr[...], preferred_element_type=F32)
    o_ref[...] = x_ref[...] + mod_ref[0, row:row + 1, :] * y


def _ffn_down(h, w2, j, x, mods, row):
    tm, tn = TM_FFN, TN_FFN2
    return pl.pallas_call(
        functools.partial(_ffn_down_kernel, row=row),
        out_shape=jax.ShapeDtypeStruct((M_TOK, D_MODEL), F32),
        grid=(D_MODEL // tn, M_TOK // tm),
        in_specs=[
            pl.BlockSpec((tm, D_FF), lambda n, m: (m, 0)),
            pl.BlockSpec((1, D_FF, tn), lambda n, m: (j, 0, n)),
            pl.BlockSpec((tm, tn), lambda n, m: (m, n)),
            pl.BlockSpec((1, MOD_ROWS, tn), lambda n, m: (_group_of_tile(m, tm), 0, n)),
        ],
        out_specs=pl.BlockSpec((tm, tn), lambda n, m: (m, n)),
        scratch_shapes=[pltpu.VMEM((D_FF, tn), BF16)],
        compiler_params=_cparams(("arbitrary", "arbitrary"), VMEM_BUDGET),
        name="ffn_down",
    )(h, w2, x, mods)


N_RUNS = N_EXPERTS + 1


def _run_blocks_pipelined(nb, a_copy, o_copy, compute):
    def body(k, carry):
        slot = k % 2

        @pl.when(k + 1 < nb)
        def _():
            a_copy(k + 1, 1 - slot).start()

        a_copy(k, slot).wait()

        @pl.when(k >= 2)
        def _():
            o_copy(k - 2, slot).wait()

        compute(slot)
        o_copy(k, slot).start()
        return carry

    lax.fori_loop(0, nb, body, 0)

    @pl.when(nb >= 2)
    def _():
        o_copy(nb - 2, nb % 2).wait()

    @pl.when(nb >= 1)
    def _():
        o_copy(nb - 1, (nb - 1) % 2).wait()


def _run_units_pipelined(nb, a_copy, o_copy, compute, after_first_start):
    nu = (nb + 1) // 2

    def by_size(k, fn):
        full = 2 * k + 1 < nb

        @pl.when(full)
        def _():
            fn(True)

        @pl.when(jnp.logical_not(full))
        def _():
            fn(False)

    @pl.when(nu >= 1)
    def _():
        by_size(0, lambda full: a_copy(0, 0, full).start())

    after_first_start()

    def body(k, carry):
        slot = k % 2

        @pl.when(k + 1 < nu)
        def _():
            by_size(k + 1, lambda full: a_copy(k + 1, 1 - slot, full).start())

        by_size(k, lambda full: a_copy(k, slot, full).wait())

        @pl.when(k >= 2)
        def _():
            o_copy(k - 2, slot, True).wait()

        def run(full):
            compute(slot, full)
            o_copy(k, slot, full).start()

        by_size(k, run)
        return carry

    lax.fori_loop(0, nu, body, 0)

    @pl.when(nu >= 2)
    def _():
        o_copy(nu - 2, nu % 2, True).wait()

    @pl.when(nu >= 1)
    def _():
        by_size(nu - 1, lambda full: o_copy(nu - 1, (nu - 1) % 2, full).wait())


def _zero_tail_blocks(nb, obuf, o_copy):
    obuf[0] = jnp.zeros(obuf.shape[1:], obuf.dtype)

    def body(k, carry):
        o_copy(k, 0).start()
        o_copy(k, 0).wait()
        return carry

    lax.fori_loop(0, nb, body, 0)


def _weights_for_step(n, r, n_tiles, w_copies):
    step = n * N_EXPERTS + r
    slot = step % 2

    @pl.when(step == 0)
    def _():
        for c in w_copies(0, 0, 0):
            c.start()

    nxt = step + 1

    @pl.when(nxt < n_tiles * N_EXPERTS)
    def _():
        for c in w_copies(nxt // N_EXPERTS, nxt % N_EXPERTS, 1 - slot):
            c.start(priority=1)

    for c in w_copies(n, r, slot):
        c.wait()
    return slot


def _moe_up_kernel(bs_ref, nb_ref, a_hbm, w1_hbm, w3_hbm, h_hbm, wbuf, s1, s3, abuf, obuf, sem_w, sem_a, sem_o, *, j):
    n = pl.program_id(0)
    r = pl.program_id(1)
    b0 = bs_ref[r]
    nb = nb_ref[r]
    tm, tn = TM_FFN, obuf.shape[2]
    col = pl.multiple_of(n * tn, tn)

    def w_copies(nn, rr, slot):
        c = pl.multiple_of(nn * tn, tn)
        return (pltpu.make_async_copy(w1_hbm.at[j, rr, :, pl.ds(c, tn)], wbuf.at[slot, 0], sem_w.at[slot]),
                pltpu.make_async_copy(w3_hbm.at[j, rr, :, pl.ds(c, tn)], wbuf.at[slot, 1], sem_w.at[slot]))

    def unit_rows(full):
        return 2 * tm if full else tm

    def a_copy(k, slot, full):
        row = pl.multiple_of((b0 + 2 * k) * tm, tm)
        rows = unit_rows(full)
        return pltpu.make_async_copy(a_hbm.at[pl.ds(row, rows)], abuf.at[slot, pl.ds(0, rows)], sem_a.at[slot])

    def o_copy(k, slot, full):
        row = pl.multiple_of((b0 + 2 * k) * tm, tm)
        rows = unit_rows(full)
        return pltpu.make_async_copy(obuf.at[slot, pl.ds(0, rows)],
                                     h_hbm.at[pl.ds(row, rows), pl.ds(col, tn)], sem_o.at[slot])

    def compute(slot, full):
        rows = unit_rows(full)
        a = abuf[slot, pl.ds(0, rows), :]
        h1 = jnp.dot(a, s1[...], preferred_element_type=F32)
        h3 = jnp.dot(a, s3[...], preferred_element_type=F32)
        obuf[slot, pl.ds(0, rows), :] = (_silu(h1) * h3).astype(BF16)

    @pl.when(r < N_EXPERTS)
    def _():
        def weights_ready():
            slot = _weights_for_step(n, r, pl.num_programs(0), w_copies)

            @pl.when(nb > 0)
            def _():
                _cast_weight(wbuf.at[slot, 0], s1, D_MODEL)
                _cast_weight(wbuf.at[slot, 1], s3, D_MODEL)

        _run_units_pipelined(nb, a_copy, o_copy, compute, weights_ready)

    @pl.when(r == N_EXPERTS)
    def _():
        def tail_copy(k, slot):
            row = pl.multiple_of((b0 + k) * tm, tm)
            return pltpu.make_async_copy(obuf.at[slot, pl.ds(0, tm)],
                                         h_hbm.at[pl.ds(row, tm), pl.ds(col, tn)], sem_o.at[slot])

        _zero_tail_blocks(nb, obuf, tail_copy)


def _moe_up(xs, w1, w3, j, run_start, run_blocks):
    tm, tn = TM_FFN, TN_FFN1
    any_spec = pl.BlockSpec(memory_space=pl.ANY)
    grid_spec = pltpu.PrefetchScalarGridSpec(
        num_scalar_prefetch=2,
        grid=(D_FF // tn, N_RUNS),
        in_specs=[any_spec, any_spec, any_spec],
        out_specs=any_spec,
        scratch_shapes=[
            pltpu.VMEM((2, 2, D_MODEL, tn), F32),
            pltpu.VMEM((D_MODEL, tn), BF16), pltpu.VMEM((D_MODEL, tn), BF16),
            pltpu.VMEM((2, 2 * tm, D_MODEL), BF16), pltpu.VMEM((2, 2 * tm, tn), BF16),
            pltpu.SemaphoreType.DMA((2,)), pltpu.SemaphoreType.DMA((2,)), pltpu.SemaphoreType.DMA((2,)),
        ],
    )
    return pl.pallas_call(
        functools.partial(_moe_up_kernel, j=j),
        out_shape=jax.ShapeDtypeStruct((N_SLOTS, D_FF), BF16),
        grid_spec=grid_spec,
        compiler_params=_cparams(("arbitrary", "arbitrary"), VMEM_BUDGET),
        name="moe_up",
    )(run_start, run_blocks, xs, w1, w3)


def _moe_down_kernel(bs_ref, nb_ref, h_hbm, w_hbm, y_hbm, wbuf, wscr, abuf, obuf, sem_w, sem_a, sem_o, *, j):
    n = pl.program_id(0)
    r = pl.program_id(1)
    b0 = bs_ref[r]
    nb = nb_ref[r]
    tm, tn = obuf.shape[1], obuf.shape[3]
    col = pl.multiple_of(n * tn, tn)

    def w_copies(nn, rr, slot):
        c = pl.multiple_of(nn * tn, tn)
        return (pltpu.make_async_copy(w_hbm.at[j, rr, :, pl.ds(c, tn)], wbuf.at[slot], sem_w.at[slot]),)

    def a_copy(k, slot):
        row = pl.multiple_of((b0 + k) * tm, tm)
        return pltpu.make_async_copy(h_hbm.at[pl.ds(row, tm)], abuf.at[slot], sem_a.at[slot])

    def o_copy(k, slot):
        row = pl.multiple_of((b0 + k) * tm, tm)
        return pltpu.make_async_copy(obuf.at[slot], y_hbm.at[pl.ds(row, tm), :, pl.ds(col, tn)], sem_o.at[slot])

    def compute(slot):
        y = jnp.dot(abuf[slot], wscr[...], preferred_element_type=F32)
        obuf[slot] = y.reshape(obuf.shape[1:])

    @pl.when(r < N_EXPERTS)
    def _():
        @pl.when(nb > 0)
        def _():
            a_copy(0, 0).start()

        slot = _weights_for_step(n, r, pl.num_programs(0), w_copies)

        @pl.when(nb > 0)
        def _():
            _cast_weight(wbuf.at[slot], wscr, D_FF)
            _run_blocks_pipelined(nb, a_copy, o_copy, compute)

    @pl.when(r == N_EXPERTS)
    def _():
        _zero_tail_blocks(nb, obuf, o_copy)


def _moe_down(h, w2, j, run_start, run_blocks):
    tm, tn = TM_FFN, TN_FFN2
    any_spec = pl.BlockSpec(memory_space=pl.ANY)
    grid_spec = pltpu.PrefetchScalarGridSpec(
        num_scalar_prefetch=2,
        grid=(D_MODEL // tn, N_RUNS),
        in_specs=[any_spec, any_spec],
        out_specs=any_spec,
        scratch_shapes=[
            pltpu.VMEM((2, D_FF, tn), F32),
            pltpu.VMEM((D_FF, tn), BF16),
            pltpu.VMEM((2, tm, D_FF), BF16), pltpu.VMEM((2, tm, 1, tn), F32),
            pltpu.SemaphoreType.DMA((2,)), pltpu.SemaphoreType.DMA((2,)), pltpu.SemaphoreType.DMA((2,)),
        ],
    )
    return pl.pallas_call(
        functools.partial(_moe_down_kernel, j=j),
        out_shape=jax.ShapeDtypeStruct((N_SLOTS, 1, D_MODEL), F32),
        grid_spec=grid_spec,
        compiler_params=_cparams(("arbitrary", "arbitrary"), VMEM_BUDGET),
        name="moe_down",
    )(run_start, run_blocks, h, w2)


def _row_copy(src_hbm, src_row, dst, dst_row, sem):
    return pltpu.make_async_copy(src_hbm.at[pl.ds(src_row, 1)], dst.at[pl.ds(dst_row, 1)], sem)


ROW_DMA_UNROLL = 8


def _for_each_row(n_rows, fn):
    def body(c, carry):
        for u in range(ROW_DMA_UNROLL):
            fn(c * ROW_DMA_UNROLL + u, u)
        return carry

    lax.fori_loop(0, n_rows // ROW_DMA_UNROLL, body, 0)


def _gather_kernel(tok_ref, u_hbm, o_ref, buf, flat, sem):
    i = pl.program_id(0)
    nb = pl.num_programs(0)
    tm = o_ref.shape[0]

    def issue(block, slot):
        _for_each_row(tm, lambda r, u: _row_copy(
            u_hbm, tok_ref[block * tm + r], buf.at[slot], r, sem.at[slot]).start(priority=u % 2))

    @pl.when(i == 0)
    def _():
        issue(0, 0)

    @pl.when(i + 1 < nb)
    def _():
        issue(i + 1, (i + 1) % 2)

    slot = i % 2
    _for_each_row(tm, lambda r, u: _row_copy(u_hbm, 0, buf.at[slot], r, sem.at[slot]).wait())
    flat[...] = buf[slot].reshape(flat.shape)
    o_ref[...] = flat[...].astype(BF16)


def _gather_rows(u, slot_token):
    tm = TM_FFN
    grid_spec = pltpu.PrefetchScalarGridSpec(
        num_scalar_prefetch=1,
        grid=(N_SLOTS // tm,),
        in_specs=[pl.BlockSpec(memory_space=pl.ANY)],
        out_specs=pl.BlockSpec((tm, D_MODEL), lambda i, tok: (i, 0)),
        scratch_shapes=[pltpu.VMEM((2, tm, 1, D_MODEL), F32), pltpu.VMEM((tm, D_MODEL), F32),
                        pltpu.SemaphoreType.DMA((2,))],
    )
    return pl.pallas_call(
        _gather_kernel,
        out_shape=jax.ShapeDtypeStruct((N_SLOTS, D_MODEL), BF16),
        grid_spec=grid_spec,
        compiler_params=_cparams(("arbitrary",), 32 * 2**20),
        name="moe_gather",
    )(slot_token, u)


def _combine_kernel(d_ref, y_hbm, x_ref, r_ref, mod_ref, *rest, row):
    o_refs, (buf, flat, sem) = rest[:-3], rest[-3:]
    i = pl.program_id(0)
    nb = pl.num_programs(0)
    tm = x_ref.shape[0]

    def issue(block, slot):
        def start_pair(r, u):
            t = block * tm + r
            _row_copy(y_hbm, d_ref[2 * t], buf.at[slot, 0], r, sem.at[slot]).start(priority=0)
            _row_copy(y_hbm, d_ref[2 * t + 1], buf.at[slot, 1], r, sem.at[slot]).start(priority=1)

        _for_each_row(tm, start_pair)

    @pl.when(i == 0)
    def _():
        issue(0, 0)

    @pl.when(i + 1 < nb)
    def _():
        issue(i + 1, (i + 1) % 2)

    slot = i % 2

    def wait_pair(r, u):
        _row_copy(y_hbm, 0, buf.at[slot, 0], r, sem.at[slot]).wait()
        _row_copy(y_hbm, 0, buf.at[slot, 1], r, sem.at[slot]).wait()

    _for_each_row(tm, wait_pair)
    for k in range(TOP_K):
        flat[k] = buf[slot, k].reshape(flat.shape[1:])
    g1 = r_ref[:, 2:3]
    g2 = r_ref[:, 3:4]
    y = flat[0] * g1 + flat[1] * g2
    res = x_ref[...] + mod_ref[0, row:row + 1, :] * y
    if len(o_refs) == 1:
        o_refs[0][...] = res
    else:
        n_ctx = M_CTX // tm

        @pl.when(i < n_ctx)
        def _():
            o_refs[0][...] = res

        @pl.when(i >= n_ctx)
        def _():
            o_refs[1][...] = res


def _moe_combine(x, y_slots, dest, route, mods, row, split_groups=False):
    tm = TM_COMB
    if split_groups:
        n_ctx = M_CTX // tm
        out_shape = (jax.ShapeDtypeStruct((M_CTX, D_MODEL), F32), jax.ShapeDtypeStruct((M_LAT, D_MODEL), F32))
        out_specs = (pl.BlockSpec((tm, D_MODEL), lambda i, d: (jnp.minimum(i, n_ctx - 1), 0)),
                     pl.BlockSpec((tm, D_MODEL), lambda i, d: (jnp.maximum(i - n_ctx, 0), 0)))
    else:
        out_shape = jax.ShapeDtypeStruct((M_TOK, D_MODEL), F32)
        out_specs = pl.BlockSpec((tm, D_MODEL), lambda i, d: (i, 0))
    grid_spec = pltpu.PrefetchScalarGridSpec(
        num_scalar_prefetch=1,
        grid=(M_TOK // tm,),
        in_specs=[
            pl.BlockSpec(memory_space=pl.ANY),
            pl.BlockSpec((tm, D_MODEL), lambda i, d: (i, 0)),
            pl.BlockSpec((tm, LANES), lambda i, d: (i, 0)),
            pl.BlockSpec((1, MOD_ROWS, D_MODEL), lambda i, d: (_group_of_tile(i, tm), 0, 0)),
        ],
        out_specs=out_specs,
        scratch_shapes=[pltpu.VMEM((2, TOP_K, tm, 1, D_MODEL), F32), pltpu.VMEM((TOP_K, tm, D_MODEL), F32),
                        pltpu.SemaphoreType.DMA((2,))],
    )
    return pl.pallas_call(
        functools.partial(_combine_kernel, row=row),
        out_shape=out_shape,
        grid_spec=grid_spec,
        compiler_params=_cparams(("arbitrary",), 44 * 2**20),
        name="moe_combine",
    )(dest.reshape(-1), y_slots, x, route, mods)


def _moe_plan(route):
    tm = TM_FFN
    expert = route[:, :TOP_K].astype(jnp.int32).reshape(-1)
    onehot = (expert[:, None] == jnp.arange(N_EXPERTS, dtype=jnp.int32)[None, :]).astype(jnp.int32)
    csum = jnp.cumsum(onehot, axis=0)
    rank = jnp.take_along_axis(csum, expert[:, None], axis=1)[:, 0] - 1
    counts = csum[-1]
    padded = (counts + tm - 1) // tm * tm
    pend = jnp.cumsum(padded)
    pstart = pend - padded
    dest = pstart[expert] + rank
    token = jnp.arange(M_TOK * TOP_K, dtype=jnp.int32) // TOP_K
    filler = jnp.arange(N_SLOTS, dtype=jnp.int32) % M_TOK
    slot_token = filler.at[dest].set(token)
    n_valid = pend[-1:] // tm
    run_start = jnp.concatenate([pstart // tm, n_valid]).astype(jnp.int32)
    run_blocks = jnp.concatenate([padded // tm, N_SLOTS_BLOCKS - n_valid]).astype(jnp.int32)
    return slot_token, dest.astype(jnp.int32).reshape(M_TOK, TOP_K), run_start, run_blocks


def kernel(x_prompt, x_sample, cache_k, cache_v, c, c_ctx, ada_w, ada_b, norm1_g, norm2_g, w_in, qn_g, kn_g, pool_w, pool_scale, rel_bias, w_bp, w_ba, w_out, ffn_w1, ffn_w3, ffn_w2, router_w, router_b, moe_w1, moe_w3, moe_w2):
    x = (x_prompt.reshape(M_CTX, D_MODEL), x_sample.reshape(M_LAT, D_MODEL))
    cond8 = jnp.zeros((MOD_ROWS, D_MODEL), F32).at[0].set(c_ctx).at[1:1 + DEC_BATCH].set(c)
    mods_all = _adaln_all(cond8, ada_w, ada_b)
    mods_all = mods_all[:, :N_GROUPS].reshape(DEPTH, N_GROUPS, 6, D_MODEL)
    mods_all = jnp.pad(mods_all, ((0, 0), (0, 0), (0, MOD_ROWS - 6), (0, 0)))
    t2_all = _bias_tables(rel_bias)

    new_k = jnp.zeros((BATCH, DEPTH, N_HEADS, SEQ, HEAD_DIM), F32)
    new_v = jnp.zeros((BATCH, DEPTH, N_HEADS, SEQ, HEAD_DIM), F32)
    for l in range(DEPTH):
        j = l // 2
        mods = mods_all[l]
        u = _norm_mod(x, norm1_g[l], mods, 0)
        proj = _in_proj(u, w_in, l)
        pool_out = _pool_mixer(proj, pool_w[l], pool_scale[l])
        attn_ctx, new_k, new_v = _ctx_attention(proj, qn_g[l], kn_g[l], l, new_k, new_v)
        attn_lat = _lat_attention(proj, cache_k, cache_v, t2_all, qn_g[l], kn_g[l], l)
        merged = _mix_merge(pool_out, attn_ctx, attn_lat, w_bp, w_ba, proj, l)
        x = _out_proj_residual(merged, w_out, l, x, mods, 2)
        if l % 2 == 0:
            u2 = _norm_mod(x, norm2_g[l], mods, 3)
            h = _ffn_up(u2, ffn_w1, ffn_w3, j)
            x = _ffn_down(h, ffn_w2, j, x, mods, 5)
        else:
            u2, route = _norm_route(x, norm2_g[l], mods, 3, router_w[j], router_b[j])
            slot_token, dest, run_start, run_blocks = _moe_plan(route)
            xs = _gather_rows(u2, slot_token)
            h = _moe_up(xs, moe_w1, moe_w3, j, run_start, run_blocks)
            y_slots = _moe_down(h, moe_w2, j, run_start, run_blocks)
            x = _moe_combine(x, y_slots, dest, route, mods, 5, split_groups=(l == DEPTH - 1))

    y_ctx, y_lat = x
    return (y_ctx.reshape(BATCH, SEQ, D_MODEL), y_lat.reshape(DEC_BATCH, DEC_SEQ, D_MODEL), new_k, new_v)
```
